```python
import jax, jax.numpy as jnp
from jax import lax
import numpy as np

D_MODEL = 4096
BATCH = 4
SEQ = 4096
DEPTH = 1

D_MIX = D_MODEL
D_ATTN = D_MIX // 2
D_POOL = D_MIX - D_ATTN
HEAD_DIM = 128
N_ATTN_HEADS = D_ATTN // HEAD_DIM
POOL_WINDOWS = (2, 4, 8, 16)
N_POOL_GROUPS = len(POOL_WINDOWS)
POOL_GROUP = D_POOL // N_POOL_GROUPS
Q_BLOCK = 128
PEER_HEADS = 8
PEER_TOPK = 16
PEER_NKEYS = 128
PEER_NEXPERTS = PEER_NKEYS * PEER_NKEYS
PEER_DQ = 256
PEER_DHALF = PEER_DQ // 2
PEER_CHUNK = 64
N_MOD = 6
EPS = 1e-6

kernel_name = 'hybrid_stickbreak_pool_peer_adaln'


def rmsnorm(x, g):
    x32 = x.astype(jnp.float32)
    y = x32 * lax.rsqrt(jnp.mean(x32 * x32, axis=-1, keepdims=True) + EPS)
    return (y * g.astype(jnp.float32)).astype(x.dtype)


def stick_breaking_attention(q, k, v):
    S = q.shape[2]
    outs = []
    for i in range(S // Q_BLOCK):
        start = i * Q_BLOCK
        end = start + Q_BLOCK
        qb = q[:, :, start:end]
        kb = k[:, :, :end]
        vb = v[:, :, :end]
        z = jnp.einsum('bhtd,bhsd->bhts', qb, kb).astype(jnp.float32) * (HEAD_DIM ** -0.5)
        q_pos = start + jnp.arange(Q_BLOCK)
        k_pos = jnp.arange(end)
        causal = k_pos[None, :] < q_pos[:, None]
        log_not = jnp.where(causal, jax.nn.log_sigmoid(-z), 0.0)
        between = lax.cumsum(log_not, axis=3, reverse=True) - log_not
        a = jnp.where(causal, jnp.exp(jax.nn.log_sigmoid(z) + between), 0.0)
        outs.append(jnp.einsum('bhts,bhsd->bhtd', a.astype(vb.dtype), vb))
    return jnp.concatenate(outs, axis=2)


def multiscale_pool(p, w_pool, s_pool):
    B, S, _ = p.shape
    p32 = p.astype(jnp.float32)
    csum = jnp.cumsum(p32, axis=1)
    pos = jnp.arange(S)
    groups = []
    for gi, w in enumerate(POOL_WINDOWS):
        sl = slice(gi * POOL_GROUP, (gi + 1) * POOL_GROUP)
        cg = csum[..., sl]
        lagged = jnp.pad(cg, ((0, 0), (w, 0), (0, 0)))[:, :S]
        count = jnp.minimum(pos + 1, w).astype(jnp.float32)[None, :, None]
        groups.append((cg - lagged) / count - p32[..., sl])
    d = jnp.stack(groups, axis=2).astype(p.dtype)
    y = jnp.einsum('bsgc,gcd->bsgd', d, w_pool).reshape(B, S, D_POOL)
    return y * s_pool


def peer_ffn(h, w_query, sub_keys_1, sub_keys_2, u_experts, v_experts):
    B, S, D = h.shape
    q = jnp.einsum('bsd,dq->bsq', h, w_query).reshape(B, S, PEER_HEADS, PEER_DQ)
    q1 = q[..., :PEER_DHALF]
    q2 = q[..., PEER_DHALF:]
    s1 = jnp.einsum('bshc,nc->bshn', q1, sub_keys_1).astype(jnp.float32)
    s2 = jnp.einsum('bshc,nc->bshn', q2, sub_keys_2).astype(jnp.float32)
    v1, i1 = lax.top_k(s1, PEER_TOPK)
    v2, i2 = lax.top_k(s2, PEER_TOPK)
    cand_s = (v1[..., :, None] + v2[..., None, :]).reshape(B, S, PEER_HEADS, PEER_TOPK * PEER_TOPK)
    cand_i = (i1[..., :, None] * PEER_NKEYS + i2[..., None, :]).reshape(B, S, PEER_HEADS, PEER_TOPK * PEER_TOPK)
    top_s, sel = lax.top_k(cand_s, PEER_TOPK)
    idx = jnp.take_along_axis(cand_i, sel, axis=-1)
    g = jax.nn.softmax(top_s, axis=-1).astype(h.dtype)
    n_chunks = (B * S) // PEER_CHUNK
    hc = h.reshape(n_chunks, PEER_CHUNK, D)
    ic = idx.reshape(n_chunks, PEER_CHUNK, PEER_HEADS, PEER_TOPK)
    gc = g.reshape(n_chunks, PEER_CHUNK, PEER_HEADS, PEER_TOPK)

    def expert_chunk(args):
        hx, ix, gx = args
        u = jnp.take(u_experts, ix, axis=0)
        act = jax.nn.gelu(jnp.einsum('chkd,cd->chk', u, hx), approximate=False) * gx
        vv = jnp.take(v_experts, ix, axis=0)
        return jnp.einsum('chk,chkd->cd', act, vv)

    y = lax.map(expert_chunk, (hc, ic, gc))
    return y.reshape(B, S, D)


def setup_inputs(seed: int = 0) -> dict:
    key = jax.random.key(seed)
    ks = jax.random.split(key, 17)
    f32 = jnp.float32
    D = D_MODEL
    nrm = lambda k, shape, s: jax.random.normal(k, shape, f32) * s
    return {
        'x': nrm(ks[0], (BATCH, SEQ, D), 1.0),
        'c': nrm(ks[1], (BATCH, D), 1.0),
        'w_ada': nrm(ks[2], (DEPTH, D, N_MOD * D), 0.5 * D ** -0.5),
        'b_ada': nrm(ks[3], (DEPTH, N_MOD * D), 0.01),
        'g_norm1': 1.0 + nrm(ks[4], (DEPTH, D), 0.02),
        'w_in': nrm(ks[5], (DEPTH, D, 3 * D_ATTN + D_POOL), D ** -0.5),
        'g_attn_head': 1.0 + nrm(ks[6], (DEPTH, N_ATTN_HEADS, HEAD_DIM), 0.02),
        'w_pool': nrm(ks[7], (DEPTH, N_POOL_GROUPS, POOL_GROUP, POOL_GROUP), POOL_GROUP ** -0.5),
        's_pool': 1.0 + nrm(ks[8], (DEPTH, D_POOL), 0.02),
        'w_out': nrm(ks[9], (DEPTH, D_MIX, D), D_MIX ** -0.5),
        'g_norm2': 1.0 + nrm(ks[10], (DEPTH, D), 0.02),
        'w_query': nrm(ks[11], (DEPTH, D, PEER_HEADS * PEER_DQ), D ** -0.5),
        'sub_keys_1': nrm(ks[12], (DEPTH, PEER_NKEYS, PEER_DHALF), PEER_DHALF ** -0.5),
        'sub_keys_2': nrm(ks[13], (DEPTH, PEER_NKEYS, PEER_DHALF), PEER_DHALF ** -0.5),
        'u_experts': nrm(ks[14], (DEPTH, PEER_NEXPERTS, D), D ** -0.5),
        'v_experts': nrm(ks[15], (DEPTH, PEER_NEXPERTS, D), 0.5),
        'g_final': 1.0 + nrm(ks[16], (D,), 0.02),
    }


def reference(x, c, w_ada, b_ada, g_norm1, w_in, g_attn_head, w_pool, s_pool, w_out,
              g_norm2, w_query, sub_keys_1, sub_keys_2, u_experts, v_experts, g_final):
    B, S, D = x.shape
    silu_c = jax.nn.silu(c)
    for l in range(DEPTH):
        mod = (jnp.einsum('bd,dm->bm', silu_c, w_ada[l]) + b_ada[l]).reshape(B, N_MOD, 1, D)
        shift1, scale1, gate1 = mod[:, 0], mod[:, 1], mod[:, 2]
        shift2, scale2, gate2 = mod[:, 3], mod[:, 4], mod[:, 5]

        h = rmsnorm(x, g_norm1[l]) * (1.0 + scale1) + shift1
        proj = jnp.einsum('bsd,de->bse', h, w_in[l])
        q = proj[..., :D_ATTN].reshape(B, S, N_ATTN_HEADS, HEAD_DIM).transpose(0, 2, 1, 3)
        k = proj[..., D_ATTN:2 * D_ATTN].reshape(B, S, N_ATTN_HEADS, HEAD_DIM).transpose(0, 2, 1, 3)
        v = proj[..., 2 * D_ATTN:3 * D_ATTN].reshape(B, S, N_ATTN_HEADS, HEAD_DIM).transpose(0, 2, 1, 3)
        p = proj[..., 3 * D_ATTN:]
        o_attn = stick_breaking_attention(q, k, v).transpose(0, 2, 1, 3)
        o_attn = rmsnorm(o_attn, g_attn_head[l]).reshape(B, S, D_ATTN)
        o_pool = multiscale_pool(p, w_pool[l], s_pool[l])
        mix = jnp.einsum('bse,ed->bsd', jnp.concatenate([o_attn, o_pool], axis=-1), w_out[l])
        x = x + gate1 * mix

        h2 = rmsnorm(x, g_norm2[l]) * (1.0 + scale2) + shift2
        x = x + gate2 * peer_ffn(h2, w_query[l], sub_keys_1[l], sub_keys_2[l], u_experts[l], v_experts[l])
    return rmsnorm(x, g_final)
```

```python
import functools

import jax
import jax.numpy as jnp
from jax import lax
from jax.experimental import pallas as pl
from jax.experimental.pallas import tpu as pltpu

F32 = jnp.float32
BF16 = jnp.bfloat16
I32 = jnp.int32

HEAD_DIM = 128
POOL_WINDOWS = (2, 4, 8, 16)
POOL_HALO = 16
PEER_HEADS = 8
PEER_TOPK = 16
PEER_NKEYS = 128
N_MOD = 6
EPS = 1e-6
MIN_SUBLANES = 8
VMEM_LIMIT = 56 * 1024 * 1024


def _params(semantics, vmem=VMEM_LIMIT):
    return pltpu.CompilerParams(dimension_semantics=semantics, vmem_limit_bytes=vmem)


def _ada_kernel(c_ref, w_ref, b_ref, o_ref):
    c = c_ref[...]
    silu = c / (1.0 + jnp.exp(-c))
    o_ref[...] = jnp.dot(silu.astype(BF16), w_ref[...].astype(BF16),
                         preferred_element_type=F32) + b_ref[...]


def _ada_mod(c, w_ada, b_ada, tn=512):
    b, d = c.shape
    n = w_ada.shape[1]
    rows = -(-b // MIN_SUBLANES) * MIN_SUBLANES
    c_pad = jnp.pad(c, ((0, rows - b), (0, 0)))
    out = pl.pallas_call(
        _ada_kernel,
        out_shape=jax.ShapeDtypeStruct((rows, n), F32),
        grid=(n // tn,),
        in_specs=[pl.BlockSpec((rows, d), lambda j: (0, 0)),
                  pl.BlockSpec((d, tn), lambda j: (0, j)),
                  pl.BlockSpec((1, tn), lambda j: (0, j))],
        out_specs=pl.BlockSpec((rows, tn), lambda j: (0, j)),
        compiler_params=_params(("arbitrary",)),
        name="ada_mod",
    )(c_pad, w_ada, b_ada.reshape(1, n))
    return out[:b]


def _norm_mod_kernel(x_ref, g_ref, sc_ref, sh_ref, o_ref):
    x = x_ref[0]
    y = x * lax.rsqrt(jnp.mean(x * x, axis=-1, keepdims=True) + EPS) * g_ref[...]
    o_ref[0] = (y * (1.0 + sc_ref[0]) + sh_ref[0]).astype(o_ref.dtype)


def _norm_mod(x, g, scale, shift, ts=512):
    b, s, d = x.shape
    return pl.pallas_call(
        _norm_mod_kernel,
        out_shape=jax.ShapeDtypeStruct((b, s, d), BF16),
        grid=(b, s // ts),
        in_specs=[pl.BlockSpec((1, ts, d), lambda i, j: (i, j, 0)),
                  pl.BlockSpec((1, d), lambda i, j: (0, 0)),
                  pl.BlockSpec((1, 1, d), lambda i, j: (i, 0, 0)),
                  pl.BlockSpec((1, 1, d), lambda i, j: (i, 0, 0))],
        out_specs=pl.BlockSpec((1, ts, d), lambda i, j: (i, j, 0)),
        compiler_params=_params(("parallel", "parallel")),
        name="norm_mod",
    )(x, g.reshape(1, d), scale, shift)


def _mm_kernel(a_ref, w_ref, o_ref):
    o_ref[...] = jnp.dot(a_ref[...], w_ref[...],
                         preferred_element_type=F32).astype(o_ref.dtype)


def _matmul(a, w, tm, tn, out_dtype=BF16, name="matmul"):
    m, k = a.shape
    n = w.shape[1]
    return pl.pallas_call(
        _mm_kernel,
        out_shape=jax.ShapeDtypeStruct((m, n), out_dtype),
        grid=(m // tm, n // tn),
        in_specs=[pl.BlockSpec((tm, k), lambda i, j: (i, 0)),
                  pl.BlockSpec((k, tn), lambda i, j: (0, j))],
        out_specs=pl.BlockSpec((tm, tn), lambda i, j: (i, j)),
        compiler_params=_params(("parallel", "arbitrary")),
        name=name,
    )(a, w)


def _attn_kernel(q_ref, k_ref, v_ref, g_ref, o_ref, *, blk, scale):
    seq = q_ref.shape[0]
    n_blk = seq // blk
    later = lax.broadcasted_iota(I32, (blk, blk), 0) > lax.broadcasted_iota(I32, (blk, blk), 1)
    suffix = jnp.where(later, 1.0, 0.0).astype(BF16)
    causal = lax.broadcasted_iota(I32, (blk, blk), 1) < lax.broadcasted_iota(I32, (blk, blk), 0)
    g = g_ref[0]

    def softplus_scores(q, kb):
        z = lax.dot_general(q, kb, (((1,), (1,)), ((), ())),
                            preferred_element_type=F32) * scale
        sp = jnp.maximum(z, 0.0) + jnp.log(1.0 + jnp.exp(-jnp.abs(z)))
        return z, sp

    def suffix_sums(log_not):
        hi = log_not.astype(BF16)
        lo = (log_not - hi.astype(F32)).astype(BF16)
        return (jnp.dot(hi, suffix, preferred_element_type=F32)
                + jnp.dot(lo, suffix, preferred_element_type=F32))

    def q_block(qi, carry):
        q0 = pl.multiple_of(qi * blk, blk)
        q = q_ref[pl.ds(q0, blk), :]
        z, sp = softplus_scores(q, k_ref[pl.ds(q0, blk), :])
        log_not = jnp.where(causal, -sp, 0.0)
        inside = suffix_sums(log_not)
        a = jnp.where(causal, jnp.exp(z - sp + inside), 0.0)
        acc = jnp.dot(a.astype(BF16), v_ref[pl.ds(q0, blk), :], preferred_element_type=F32)
        run = inside[:, 0:1] + log_not[:, 0:1]

        def k_block(it, c):
            acc, run = c
            k0 = pl.multiple_of((qi - 1 - it) * blk, blk)
            z, sp = softplus_scores(q, k_ref[pl.ds(k0, blk), :])
            log_not = -sp
            inside = suffix_sums(log_not)
            a = jnp.exp(z - sp + inside + run)
            acc = acc + jnp.dot(a.astype(BF16), v_ref[pl.ds(k0, blk), :],
                                preferred_element_type=F32)
            run = run + inside[:, 0:1] + log_not[:, 0:1]
            return acc, run

        acc, run = lax.fori_loop(0, qi, k_block, (acc, run))
        y = acc * lax.rsqrt(jnp.mean(acc * acc, axis=-1, keepdims=True) + EPS) * g
        o_ref[pl.ds(q0, blk), :] = y.astype(o_ref.dtype)
        return carry

    lax.fori_loop(0, n_blk, q_block, 0)


def _attention(proj, g_head, batch, seq, n_heads, blk=256):
    n_tok = proj.shape[0]
    d_attn = n_heads * HEAD_DIM
    return pl.pallas_call(
        functools.partial(_attn_kernel, blk=blk, scale=HEAD_DIM ** -0.5),
        out_shape=jax.ShapeDtypeStruct((n_tok, d_attn), BF16),
        grid=(batch, n_heads),
        in_specs=[pl.BlockSpec((seq, HEAD_DIM), lambda b, h: (b, h)),
                  pl.BlockSpec((seq, HEAD_DIM), lambda b, h: (b, n_heads + h)),
                  pl.BlockSpec((seq, HEAD_DIM), lambda b, h: (b, 2 * n_heads + h)),
                  pl.BlockSpec((1, 1, HEAD_DIM), lambda b, h: (h, 0, 0))],
        out_specs=pl.BlockSpec((seq, HEAD_DIM), lambda b, h: (b, h)),
        compiler_params=_params(("parallel", "parallel")),
        name="stickbreak_attn",
    )(proj, proj, proj, g_head.reshape(n_heads, 1, HEAD_DIM))


def _pool_kernel(p_ref, halo_ref, w_ref, s_ref, o_ref, *, ts, seq_tiles, group):
    tile = pl.program_id(0) % seq_tiles
    p = p_ref[...].astype(F32)
    halo = jnp.where(tile == 0, 0.0, halo_ref[...].astype(F32))
    ext = jnp.concatenate([halo, p], axis=0)
    pos = tile * ts + lax.broadcasted_iota(I32, (ts, 1), 0)
    for gi, w in enumerate(POOL_WINDOWS):
        cols = slice(gi * group, (gi + 1) * group)
        tot = ext[:, cols]
        span = 1
        while span < w:
            tot = tot + pltpu.roll(tot, span, 0)
            span *= 2
        count = jnp.minimum(pos + 1, w).astype(F32)
        d = tot[POOL_HALO:] / count - p[:, cols]
        y = jnp.dot(d.astype(BF16), w_ref[gi], preferred_element_type=F32) * s_ref[:, cols]
        o_ref[:, cols] = y.astype(o_ref.dtype)


def _pool(proj, w_pool, s_pool, seq, col_block, ts=512):
    n_tok = proj.shape[0]
    n_groups, group, _ = w_pool.shape
    d_pool = n_groups * group
    seq_tiles = seq // ts
    halo_per_tile = ts // POOL_HALO
    return pl.pallas_call(
        functools.partial(_pool_kernel, ts=ts, seq_tiles=seq_tiles, group=group),
        out_shape=jax.ShapeDtypeStruct((n_tok, d_pool), BF16),
        grid=(n_tok // ts,),
        in_specs=[pl.BlockSpec((ts, d_pool), lambda i: (i, col_block)),
                  pl.BlockSpec((POOL_HALO, d_pool),
                               lambda i: (jnp.maximum(i * halo_per_tile - 1, 0), col_block)),
                  pl.BlockSpec((n_groups, group, group), lambda i: (0, 0, 0)),
                  pl.BlockSpec((1, d_pool), lambda i: (0, 0))],
        out_specs=pl.BlockSpec((ts, d_pool), lambda i: (i, 0)),
        compiler_params=_params(("parallel",)),
        name="multiscale_pool",
    )(proj, proj, w_pool, s_pool.reshape(1, d_pool))


def _out_proj_kernel(a1_ref, a2_ref, w1_ref, w2_ref, x_ref, gate_ref, o_ref):
    mix = (jnp.dot(a1_ref[...], w1_ref[...], preferred_element_type=F32)
           + jnp.dot(a2_ref[...], w2_ref[...], preferred_element_type=F32))
    o_ref[...] = x_ref[...] + gate_ref[0] * mix


def _out_proj(o_attn, o_pool, w_out, x2d, gate, seq, tm=512, tn=1024):
    m, d = x2d.shape
    k1, k2 = o_attn.shape[1], o_pool.shape[1]
    assert k1 == k2
    return pl.pallas_call(
        _out_proj_kernel,
        out_shape=jax.ShapeDtypeStruct((m, d), F32),
        grid=(m // tm, d // tn),
        in_specs=[pl.BlockSpec((tm, k1), lambda i, j: (i, 0)),
                  pl.BlockSpec((tm, k2), lambda i, j: (i, 0)),
                  pl.BlockSpec((k1, tn), lambda i, j: (0, j)),
                  pl.BlockSpec((k2, tn), lambda i, j: (1, j)),
                  pl.BlockSpec((tm, tn), lambda i, j: (i, j)),
                  pl.BlockSpec((1, 1, tn), lambda i, j: ((i * tm) // seq, 0, j))],
        out_specs=pl.BlockSpec((tm, tn), lambda i, j: (i, j)),
        compiler_params=_params(("parallel", "arbitrary")),
        name="out_proj",
    )(o_attn, o_pool, w_out, w_out, x2d, gate)


def _top_rows(vals, k):
    n_rows, width = vals.shape
    rows = lax.broadcasted_iota(I32, (n_rows, width), 0)
    slot = lax.broadcasted_iota(I32, (k, width), 0)
    out_v = jnp.zeros((k, width), F32)
    out_i = jnp.zeros((k, width), I32)
    for r in range(k):
        m = jnp.max(vals, axis=0, keepdims=True)
        idx = jnp.min(jnp.where(vals == m, rows, n_rows), axis=0, keepdims=True)
        out_v = jnp.where(slot == r, m, out_v)
        out_i = jnp.where(slot == r, idx, out_i)
        vals = jnp.where(rows == idx, -jnp.inf, vals)
    return out_v, out_i, vals


def _pick_rows(table, sel):
    out = jnp.zeros(sel.shape, table.dtype)
    for r in range(table.shape[0]):
        out = jnp.where(sel == r, table[r:r + 1], out)
    return out


def _retrieve_kernel(q_ref, k1_ref, k2_ref,
                     s1_ref, c1_ref, s2_ref, e2_ref, tau_ref, tie_ref, idx_ref, gate_ref):
    half = PEER_NKEYS
    nt = (((1,), (1,)), ((), ()))
    for h in range(PEER_HEADS):
        q1 = q_ref[:, (2 * h) * half:(2 * h + 1) * half]
        q2 = q_ref[:, (2 * h + 1) * half:(2 * h + 2) * half]
        s1 = lax.dot_general(k1_ref[...], q1, nt, preferred_element_type=F32)
        s2 = lax.dot_general(k2_ref[...], q2, nt, preferred_element_type=F32)
        v1, i1, rest1 = _top_rows(s1, PEER_TOPK)
        v2, i2, rest2 = _top_rows(s2, PEER_TOPK)
        cand = jnp.concatenate([v1[r:r + 1] + v2 for r in range(PEER_TOPK)], axis=0)
        top_s, pos, rest_c = _top_rows(cand, PEER_TOPK)
        tau = top_s[PEER_TOPK - 1:PEER_TOPK]
        z = jnp.exp(top_s - top_s[0:1])
        denom = jnp.sum(z, axis=0, keepdims=True)
        next1 = jnp.max(rest1, axis=0, keepdims=True)
        next2 = jnp.max(rest2, axis=0, keepdims=True)
        next_c = jnp.max(rest_c, axis=0, keepdims=True)
        tie = (next_c >= tau) | (next1 + v2[0:1] >= tau) | (v1[0:1] + next2 >= tau)

        s1_ref[h] = s1
        s2_ref[h] = s2
        c1_ref[h] = jnp.exp(s1 - v1[0:1])
        e2_ref[h] = jnp.exp(s2 - v2[0:1]) / denom
        tau_ref[h:h + 1, :] = tau
        tie_ref[h:h + 1, :] = jnp.where(tie, 1, 0).astype(I32)
        idx_ref[h] = (_pick_rows(i1, pos // PEER_TOPK) * PEER_NKEYS
                      + _pick_rows(i2, pos % PEER_TOPK))
        gate_ref[h] = z / denom


def _retrieve(q, keys1, keys2, tt=256):
    n_tok, qw = q.shape
    kshape = (PEER_HEADS, PEER_NKEYS, n_tok)
    lshape = (PEER_HEADS, PEER_TOPK, n_tok)
    kspec = pl.BlockSpec((PEER_HEADS, PEER_NKEYS, tt), lambda i: (0, 0, i))
    lspec = pl.BlockSpec((PEER_HEADS, PEER_TOPK, tt), lambda i: (0, 0, i))
    hspec = pl.BlockSpec((PEER_HEADS, tt), lambda i: (0, i))
    return pl.pallas_call(
        _retrieve_kernel,
        out_shape=(jax.ShapeDtypeStruct(kshape, F32), jax.ShapeDtypeStruct(kshape, F32),
                   jax.ShapeDtypeStruct(kshape, F32), jax.ShapeDtypeStruct(kshape, F32),
                   jax.ShapeDtypeStruct((PEER_HEADS, n_tok), F32),
                   jax.ShapeDtypeStruct((PEER_HEADS, n_tok), I32),
                   jax.ShapeDtypeStruct(lshape, I32), jax.ShapeDtypeStruct(lshape, F32)),
        grid=(n_tok // tt,),
        in_specs=[pl.BlockSpec((tt, qw), lambda i: (i, 0)),
                  pl.BlockSpec((PEER_NKEYS, PEER_NKEYS), lambda i: (0, 0)),
                  pl.BlockSpec((PEER_NKEYS, PEER_NKEYS), lambda i: (0, 0))],
        out_specs=(kspec, kspec, kspec, kspec, hspec, hspec, lspec, lspec),
        compiler_params=_params(("parallel",)),
        name="peer_retrieve",
    )(q, keys1, keys2)


def _gelu(x):
    return 0.5 * x * (1.0 + lax.erf(x * (2.0 ** -0.5)))


def _peer_kernel(tie_ref, h_ref, u_ref, vt_ref, s1_ref, c1_ref, s2_ref, e2_ref, tau_ref,
                 idx_ref, gate_ref, o_ref, *, n_i):
    t = pl.program_id(0)
    e = pl.program_id(1)
    width = h_ref.shape[0]

    @pl.when(e == 0)
    def _():
        o_ref[...] = jnp.zeros_like(o_ref)

    def gates_by_threshold(ii):
        w = jnp.zeros((PEER_NKEYS, width), F32)
        for h in range(PEER_HEADS):
            score = s2_ref[h] + s1_ref[ii, h:h + 1, :]
            w = w + jnp.where(score >= tau_ref[h:h + 1, :], e2_ref[h], 0.0) * c1_ref[ii, h:h + 1, :]
        return w

    def gates_by_list(ii):
        ids = ((e * n_i + ii) * PEER_NKEYS
               + lax.broadcasted_iota(I32, (PEER_NKEYS, width), 0))
        w = jnp.zeros((PEER_NKEYS, width), F32)
        for h in range(PEER_HEADS):
            def slot(k, w):
                hit = ids == idx_ref[h, pl.ds(k, 1), :]
                return w + jnp.where(hit, gate_ref[h, pl.ds(k, 1), :], 0.0)
            w = lax.fori_loop(0, PEER_TOPK, slot, w)
        return w

    def step(gates):
        s = lax.dot_general(u_ref[...], h_ref[...], (((1,), (1,)), ((), ())),
                            preferred_element_type=F32)
        act = jnp.concatenate(
            [(_gelu(s[ii * PEER_NKEYS:(ii + 1) * PEER_NKEYS]) * gates(ii)).astype(BF16)
             for ii in range(n_i)], axis=0)
        o_ref[...] += jnp.dot(vt_ref[...], act, preferred_element_type=F32)

    @pl.when(tie_ref[t] == 0)
    def _():
        step(gates_by_threshold)

    @pl.when(tie_ref[t] != 0)
    def _():
        step(gates_by_list)


def _peer(tile_ties, h2, u_bf, vt_bf, s1, c1, s2, e2, tau, idx, gate, tt, te):
    n_tok, d = h2.shape
    n_exp = u_bf.shape[0]
    n_i = te // PEER_NKEYS
    kspec = pl.BlockSpec((PEER_HEADS, PEER_NKEYS, tt), lambda t, e, ties: (0, 0, t))
    ispec = pl.BlockSpec((n_i, PEER_HEADS, tt), lambda t, e, ties: (e, 0, t))
    lspec = pl.BlockSpec((PEER_HEADS, PEER_TOPK, tt), lambda t, e, ties: (0, 0, t))
    grid_spec = pltpu.PrefetchScalarGridSpec(
        num_scalar_prefetch=1,
        grid=(n_tok // tt, n_exp // te),
        in_specs=[pl.BlockSpec((tt, d), lambda t, e, ties: (t, 0)),
                  pl.BlockSpec((te, d), lambda t, e, ties: (e, 0)),
                  pl.BlockSpec((d, te), lambda t, e, ties: (0, e)),
                  ispec, ispec, kspec, kspec,
                  pl.BlockSpec((PEER_HEADS, tt), lambda t, e, ties: (0, t)),
                  lspec, lspec],
        out_specs=pl.BlockSpec((d, tt), lambda t, e, ties: (0, t)),
    )
    return pl.pallas_call(
        functools.partial(_peer_kernel, n_i=n_i),
        out_shape=jax.ShapeDtypeStruct((d, n_tok), F32),
        grid_spec=grid_spec,
        compiler_params=_params(("parallel", "arbitrary")),
        name="peer_experts",
    )(tile_ties, h2, u_bf, vt_bf, s1, c1, s2, e2, tau, idx, gate)


def _final_kernel(x_ref, yt_ref, gate_ref, g_ref, o_ref):
    x = x_ref[...] + gate_ref[0] * yt_ref[...].T
    o_ref[...] = x * lax.rsqrt(jnp.mean(x * x, axis=-1, keepdims=True) + EPS) * g_ref[...]


def _final(x1, y_t, gate, g_final, seq, ts=256):
    m, d = x1.shape
    return pl.pallas_call(
        _final_kernel,
        out_shape=jax.ShapeDtypeStruct((m, d), F32),
        grid=(m // ts,),
        in_specs=[pl.BlockSpec((ts, d), lambda i: (i, 0)),
                  pl.BlockSpec((d, ts), lambda i: (0, i)),
                  pl.BlockSpec((1, 1, d), lambda i: ((i * ts) // seq, 0, 0)),
                  pl.BlockSpec((1, d), lambda i: (0, 0))],
        out_specs=pl.BlockSpec((ts, d), lambda i: (i, 0)),
        compiler_params=_params(("parallel",)),
        name="final_norm",
    )(x1, y_t, gate, g_final.reshape(1, d))


def _layer(x, silu_mod, g_norm1, w_in, g_attn_head, w_pool, s_pool, w_out, g_norm2, w_query,
           sub_keys_1, sub_keys_2, u_experts, v_experts, *, peer_tt, peer_te):
    b, s, d = x.shape
    n_tok = b * s
    n_heads = g_attn_head.shape[0]
    d_attn = n_heads * HEAD_DIM
    mod = silu_mod.reshape(b, N_MOD, 1, d)
    shift1, scale1, gate1, shift2, scale2, gate2 = (mod[:, i] for i in range(N_MOD))

    h = _norm_mod(x, g_norm1, scale1, shift1).reshape(n_tok, d)
    proj = _matmul(h, w_in.astype(BF16), tm=1024, tn=512, name="in_proj")
    o_attn = _attention(proj, g_attn_head, b, s, n_heads)
    o_pool = _pool(proj, w_pool.astype(BF16), s_pool, s, col_block=(3 * d_attn) // (d - d_attn))
    x1 = _out_proj(o_attn, o_pool, w_out.astype(BF16), x.reshape(n_tok, d), gate1, s)

    h2 = _norm_mod(x1.reshape(b, s, d), g_norm2, scale2, shift2).reshape(n_tok, d)
    q = _matmul(h2, w_query.astype(BF16), tm=1024, tn=512, name="peer_query")
    s1, c1, s2, e2, tau, tie, idx, gate = _retrieve(
        q, sub_keys_1.astype(BF16), sub_keys_2.astype(BF16))
    tile_ties = jnp.max(tie.reshape(PEER_HEADS, n_tok // peer_tt, peer_tt), axis=(0, 2))
    y_t = _peer(tile_ties, h2, u_experts.astype(BF16), v_experts.T.astype(BF16),
                jnp.transpose(s1, (1, 0, 2)), jnp.transpose(c1, (1, 0, 2)), s2, e2, tau,
                idx, gate, peer_tt, peer_te)
    return x1, y_t, gate2


def kernel(x, c, w_ada, b_ada, g_norm1, w_in, g_attn_head, w_pool, s_pool, w_out, g_norm2,
           w_query, sub_keys_1, sub_keys_2, u_experts, v_experts, g_final):
    b, s, d = x.shape
    depth = w_ada.shape[0]
    assert depth == 1, "the fused final norm assumes a single layer"
    mod = _ada_mod(c, w_ada[0], b_ada[0])
    x1, y_t, gate2 = _layer(x, mod, g_norm1[0], w_in[0], g_attn_head[0], w_pool[0], s_pool[0],
                            w_out[0], g_norm2[0], w_query[0], sub_keys_1[0], sub_keys_2[0],
                            u_experts[0], v_experts[0], peer_tt=512, peer_te=256)
    out = _final(x1, y_t, gate2, g_final, s)
    return out.reshape(b, s, d)
```

```python
import functools

import jax
import jax.numpy as jnp
from jax import lax
from jax.experimental import pallas as pl
from jax.experimental.pallas import tpu as pltpu

F32 = jnp.float32
BF16 = jnp.bfloat16
I32 = jnp.int32

HEAD_DIM = 128
POOL_WINDOWS = (2, 4, 8, 16)
POOL_HALO = 16
PEER_HEADS = 8
PEER_TOPK = 16
PEER_NKEYS = 128
PEER_CHUNK_KEYS = 2
PEER_TOK_CHUNK = 256
PEER_TT = 512
PEER_TE = 512
N_MOD = 6
EPS = 1e-6
MIN_SUBLANES = 8
VMEM_LIMIT = 56 * 1024 * 1024


def _params(semantics, vmem=VMEM_LIMIT):
    return pltpu.CompilerParams(dimension_semantics=semantics, vmem_limit_bytes=vmem)


def _ada_kernel(c_ref, w_ref, b_ref, o_ref):
    c = c_ref[...]
    silu = c / (1.0 + jnp.exp(-c))
    o_ref[...] = jnp.dot(silu.astype(BF16), w_ref[...].astype(BF16),
                         preferred_element_type=F32) + b_ref[...]


def _ada_mod(c, w_ada, b_ada, tn=512):
    b, d = c.shape
    n = w_ada.shape[1]
    rows = -(-b // MIN_SUBLANES) * MIN_SUBLANES
    c_pad = jnp.pad(c, ((0, rows - b), (0, 0)))
    out = pl.pallas_call(
        _ada_kernel,
        out_shape=jax.ShapeDtypeStruct((rows, n), F32),
        grid=(n // tn,),
        in_specs=[pl.BlockSpec((rows, d), lambda j: (0, 0)),
                  pl.BlockSpec((d, tn), lambda j: (0, j)),
                  pl.BlockSpec((1, tn), lambda j: (0, j))],
        out_specs=pl.BlockSpec((rows, tn), lambda j: (0, j)),
        compiler_params=_params(("arbitrary",)),
        name="ada_mod",
    )(c_pad, w_ada, b_ada.reshape(1, n))
    return out[:b]


def _norm_mod_kernel(x_ref, g_ref, sc_ref, sh_ref, o_ref):
    x = x_ref[0]
    y = x * lax.rsqrt(jnp.mean(x * x, axis=-1, keepdims=True) + EPS) * g_ref[...]
    o_ref[0] = (y * (1.0 + sc_ref[0]) + sh_ref[0]).astype(o_ref.dtype)


def _norm_mod(x, g, scale, shift, ts=512):
    b, s, d = x.shape
    return pl.pallas_call(
        _norm_mod_kernel,
        out_shape=jax.ShapeDtypeStruct((b, s, d), BF16),
        grid=(b, s // ts),
        in_specs=[pl.BlockSpec((1, ts, d), lambda i, j: (i, j, 0)),
                  pl.BlockSpec((1, d), lambda i, j: (0, 0)),
                  pl.BlockSpec((1, 1, d), lambda i, j: (i, 0, 0)),
                  pl.BlockSpec((1, 1, d), lambda i, j: (i, 0, 0))],
        out_specs=pl.BlockSpec((1, ts, d), lambda i, j: (i, j, 0)),
        compiler_params=_params(("parallel", "parallel")),
        name="norm_mod",
    )(x, g.reshape(1, d), scale, shift)


def _mm_kernel(a_ref, w_ref, o_ref):
    o_ref[...] = jnp.dot(a_ref[...], w_ref[...],
                         preferred_element_type=F32).astype(o_ref.dtype)


def _matmul(a, w, tm, tn, out_dtype=BF16, name="matmul"):
    m, k = a.shape
    n = w.shape[1]
    return pl.pallas_call(
        _mm_kernel,
        out_shape=jax.ShapeDtypeStruct((m, n), out_dtype),
        grid=(m // tm, n // tn),
        in_specs=[pl.BlockSpec((tm, k), lambda i, j: (i, 0)),
                  pl.BlockSpec((k, tn), lambda i, j: (0, j))],
        out_specs=pl.BlockSpec((tm, tn), lambda i, j: (i, j)),
        compiler_params=_params(("parallel", "arbitrary")),
        name=name,
    )(a, w)


_SIGN_BIT = 0x80000000
_BF16_BITS = 0xFFFF0000
_LOG2_E = 1.4426950408889634


def _attn_kernel(q_ref, k_ref, v_ref, g_ref, o_ref, vt_scr, *, tq, tk, neg_scale2):
    seq = q_ref.shape[0]
    for j in range(seq // tk):
        vt_scr[j] = v_ref[j * tk:(j + 1) * tk, :].astype(F32).T.astype(BF16)
    row = lax.broadcasted_iota(I32, (tk, 2 * tk), 0)
    col = lax.broadcasted_iota(I32, (tk, 2 * tk), 1)
    col = jnp.where(col >= tk, col - tk, col)
    suffix2 = jnp.where(col >= row, 1.0, 0.0).astype(BF16)
    key_minus_query = (lax.broadcasted_iota(I32, (tk, tq), 0)
                       - lax.broadcasted_iota(I32, (tk, tq), 1))
    g = g_ref[0]

    def tile(q, kt, run, acc, valid):
        kb = k_ref[pl.ds(pl.multiple_of(kt * tk, tk), tk), :]
        nz = lax.dot_general(kb, q, (((1,), (1,)), ((), ())),
                             preferred_element_type=F32) * neg_scale2
        neg_abs = lax.bitcast_convert_type(
            lax.bitcast_convert_type(nz, jnp.uint32) | jnp.uint32(_SIGN_BIT), F32)
        log_not = jnp.minimum(nz, 0.0) - jnp.log2(1.0 + jnp.exp2(neg_abs))
        if valid is not None:
            log_not = jnp.where(valid, log_not, 0.0)
        hi = lax.bitcast_convert_type(
            lax.bitcast_convert_type(log_not, jnp.uint32) & jnp.uint32(_BF16_BITS), F32)
        both = jnp.concatenate([hi.astype(BF16), (log_not - hi).astype(BF16)], axis=0)
        incl = jnp.dot(suffix2, both, preferred_element_type=F32)
        a = jnp.exp2(incl + run - nz)
        if valid is not None:
            a = jnp.where(valid, a, 0.0)
        acc = acc + jnp.dot(vt_scr[kt], a.astype(BF16), preferred_element_type=F32)
        return run + incl[0:1, :], acc

    def q_block(qi, carry):
        q0 = pl.multiple_of(qi * tq, tq)
        q = q_ref[pl.ds(q0, tq), :]
        run = jnp.zeros((1, tq), F32)
        acc = jnp.zeros((HEAD_DIM, tq), F32)
        first_kt = qi * (tq // tk)
        for d in reversed(range(tq // tk)):
            valid = key_minus_query < -d * tk
            run, acc = tile(q, first_kt + d, run, acc, valid)

        def k_tile(it, c):
            return tile(q, first_kt - 1 - it, c[0], c[1], None)

        run, acc = lax.fori_loop(0, first_kt, k_tile, (run, acc))
        y = acc * lax.rsqrt(jnp.mean(acc * acc, axis=0, keepdims=True) + EPS) * g
        o_ref[pl.ds(q0, tq), :] = y.T.astype(o_ref.dtype)
        return carry

    lax.fori_loop(0, seq // tq, q_block, 0)


def _attention(proj, g_head, batch, seq, n_heads, tq=512, tk=256):
    n_tok = proj.shape[0]
    d_attn = n_heads * HEAD_DIM
    return pl.pallas_call(
        functools.partial(_attn_kernel, tq=tq, tk=tk,
                          neg_scale2=-(HEAD_DIM ** -0.5) * _LOG2_E),
        out_shape=jax.ShapeDtypeStruct((n_tok, d_attn), BF16),
        grid=(batch, n_heads),
        in_specs=[pl.BlockSpec((seq, HEAD_DIM), lambda b, h: (b, h)),
                  pl.BlockSpec((seq, HEAD_DIM), lambda b, h: (b, n_heads + h)),
                  pl.BlockSpec((seq, HEAD_DIM), lambda b, h: (b, 2 * n_heads + h)),
                  pl.BlockSpec((1, HEAD_DIM, 1), lambda b, h: (h, 0, 0))],
        out_specs=pl.BlockSpec((seq, HEAD_DIM), lambda b, h: (b, h)),
        scratch_shapes=[pltpu.VMEM((seq // tk, HEAD_DIM, tk), BF16)],
        compiler_params=_params(("parallel", "parallel")),
        name="stickbreak_attn",
    )(proj, proj, proj, g_head.reshape(n_heads, HEAD_DIM, 1))


def _pool_kernel(p_ref, halo_ref, w_ref, s_ref, o_ref, *, ts, seq_tiles, group):
    tile = pl.program_id(0) % seq_tiles
    p = p_ref[...].astype(F32)
    halo = jnp.where(tile == 0, 0.0, halo_ref[...].astype(F32))
    ext = jnp.concatenate([halo, p], axis=0)
    pos = tile * ts + lax.broadcasted_iota(I32, (ts, 1), 0)
    for gi, w in enumerate(POOL_WINDOWS):
        cols = slice(gi * group, (gi + 1) * group)
        tot = ext[:, cols]
        span = 1
        while span < w:
            tot = tot + pltpu.roll(tot, span, 0)
            span *= 2
        count = jnp.minimum(pos + 1, w).astype(F32)
        d = tot[POOL_HALO:] / count - p[:, cols]
        y = jnp.dot(d.astype(BF16), w_ref[gi], preferred_element_type=F32) * s_ref[:, cols]
        o_ref[:, cols] = y.astype(o_ref.dtype)


def _pool(proj, w_pool, s_pool, seq, col_block, ts=512):
    n_tok = proj.shape[0]
    n_groups, group, _ = w_pool.shape
    d_pool = n_groups * group
    seq_tiles = seq // ts
    halo_per_tile = ts // POOL_HALO
    return pl.pallas_call(
        functools.partial(_pool_kernel, ts=ts, seq_tiles=seq_tiles, group=group),
        out_shape=jax.ShapeDtypeStruct((n_tok, d_pool), BF16),
        grid=(n_tok // ts,),
        in_specs=[pl.BlockSpec((ts, d_pool), lambda i: (i, col_block)),
                  pl.BlockSpec((POOL_HALO, d_pool),
                               lambda i: (jnp.maximum(i * halo_per_tile - 1, 0), col_block)),
                  pl.BlockSpec((n_groups, group, group), lambda i: (0, 0, 0)),
                  pl.BlockSpec((1, d_pool), lambda i: (0, 0))],
        out_specs=pl.BlockSpec((ts, d_pool), lambda i: (i, 0)),
        compiler_params=_params(("parallel",)),
        name="multiscale_pool",
    )(proj, proj, w_pool, s_pool.reshape(1, d_pool))


def _out_proj_kernel(a1_ref, a2_ref, w1_ref, w2_ref, x_ref, gate_ref, o_ref):
    mix = (jnp.dot(a1_ref[...], w1_ref[...], preferred_element_type=F32)
           + jnp.dot(a2_ref[...], w2_ref[...], preferred_element_type=F32))
    o_ref[...] = x_ref[...] + gate_ref[0] * mix


def _out_proj(o_attn, o_pool, w_out, x2d, gate, seq, tm=512, tn=1024):
    m, d = x2d.shape
    k1, k2 = o_attn.shape[1], o_pool.shape[1]
    assert k1 == k2
    return pl.pallas_call(
        _out_proj_kernel,
        out_shape=jax.ShapeDtypeStruct((m, d), F32),
        grid=(m // tm, d // tn),
        in_specs=[pl.BlockSpec((tm, k1), lambda i, j: (i, 0)),
                  pl.BlockSpec((tm, k2), lambda i, j: (i, 0)),
                  pl.BlockSpec((k1, tn), lambda i, j: (0, j)),
                  pl.BlockSpec((k2, tn), lambda i, j: (1, j)),
                  pl.BlockSpec((tm, tn), lambda i, j: (i, j)),
                  pl.BlockSpec((1, 1, tn), lambda i, j: ((i * tm) // seq, 0, j))],
        out_specs=pl.BlockSpec((tm, tn), lambda i, j: (i, j)),
        compiler_params=_params(("parallel", "arbitrary")),
        name="out_proj",
    )(o_attn, o_pool, w_out, w_out, x2d, gate)


def _top_rows(vals, k):
    n_rows, width = vals.shape
    rows = lax.broadcasted_iota(I32, (n_rows, width), 0)
    slot = lax.broadcasted_iota(I32, (k, width), 0)
    out_v = jnp.zeros((k, width), F32)
    out_i = jnp.zeros((k, width), I32)
    for r in range(k):
        m = jnp.max(vals, axis=0, keepdims=True)
        idx = jnp.min(jnp.where(vals == m, rows, n_rows), axis=0, keepdims=True)
        out_v = jnp.where(slot == r, m, out_v)
        out_i = jnp.where(slot == r, idx, out_i)
        vals = jnp.where(rows == idx, -jnp.inf, vals)
    return out_v, out_i, vals


def _pick_rows(table, sel):
    out = jnp.zeros(sel.shape, table.dtype)
    for r in range(table.shape[0]):
        out = jnp.where(sel == r, table[r:r + 1], out)
    return out


def _retrieve_kernel(q_ref, k1_ref, k2_ref,
                     s1_ref, c1_ref, s2_ref, e2_ref, tau_ref, tie_ref, idx_ref, gate_ref):
    half = PEER_NKEYS
    nt = (((1,), (1,)), ((), ()))
    for h in range(PEER_HEADS):
        q1 = q_ref[:, (2 * h) * half:(2 * h + 1) * half]
        q2 = q_ref[:, (2 * h + 1) * half:(2 * h + 2) * half]
        s1 = lax.dot_general(k1_ref[...], q1, nt, preferred_element_type=F32)
        s2 = lax.dot_general(k2_ref[...], q2, nt, preferred_element_type=F32)
        v1, i1, rest1 = _top_rows(s1, PEER_TOPK)
        v2, i2, rest2 = _top_rows(s2, PEER_TOPK)
        cand = jnp.concatenate([v1[r:r + 1] + v2 for r in range(PEER_TOPK)], axis=0)
        top_s, pos, rest_c = _top_rows(cand, PEER_TOPK)
        tau = top_s[PEER_TOPK - 1:PEER_TOPK]
        z = jnp.exp(top_s - top_s[0:1])
        denom = jnp.sum(z, axis=0, keepdims=True)
        next1 = jnp.max(rest1, axis=0, keepdims=True)
        next2 = jnp.max(rest2, axis=0, keepdims=True)
        next_c = jnp.max(rest_c, axis=0, keepdims=True)
        tie = (next_c >= tau) | (next1 + v2[0:1] >= tau) | (v1[0:1] + next2 >= tau)

        s1_ref[h] = s1
        s2_ref[h] = s2
        c1_ref[h] = jnp.exp(s1 - v1[0:1])
        e2_ref[h] = jnp.exp(s2 - v2[0:1]) / denom
        tau_ref[h:h + 1, :] = tau
        tie_ref[h:h + 1, :] = jnp.where(tie, 1, 0).astype(I32)
        idx_ref[h] = (_pick_rows(i1, pos // PEER_TOPK) * PEER_NKEYS
                      + _pick_rows(i2, pos % PEER_TOPK))
        gate_ref[h] = z / denom


def _retrieve(q, keys1, keys2, tt=256):
    n_tok, qw = q.shape
    kshape = (PEER_HEADS, PEER_NKEYS, n_tok)
    lshape = (PEER_HEADS, PEER_TOPK, n_tok)
    kspec = pl.BlockSpec((PEER_HEADS, PEER_NKEYS, tt), lambda i: (0, 0, i))
    lspec = pl.BlockSpec((PEER_HEADS, PEER_TOPK, tt), lambda i: (0, 0, i))
    hspec = pl.BlockSpec((PEER_HEADS, tt), lambda i: (0, i))
    return pl.pallas_call(
        _retrieve_kernel,
        out_shape=(jax.ShapeDtypeStruct(kshape, F32), jax.ShapeDtypeStruct(kshape, F32),
                   jax.ShapeDtypeStruct(kshape, F32), jax.ShapeDtypeStruct(kshape, F32),
                   jax.ShapeDtypeStruct((PEER_HEADS, n_tok), F32),
                   jax.ShapeDtypeStruct((PEER_HEADS, n_tok), I32),
                   jax.ShapeDtypeStruct(lshape, I32), jax.ShapeDtypeStruct(lshape, F32)),
        grid=(n_tok // tt,),
        in_specs=[pl.BlockSpec((tt, qw), lambda i: (i, 0)),
                  pl.BlockSpec((PEER_NKEYS, PEER_NKEYS), lambda i: (0, 0)),
                  pl.BlockSpec((PEER_NKEYS, PEER_NKEYS), lambda i: (0, 0))],
        out_specs=(kspec, kspec, kspec, kspec, hspec, hspec, lspec, lspec),
        compiler_params=_params(("parallel",)),
        name="peer_retrieve",
    )(q, keys1, keys2)


def _gelu(x):
    return 0.5 * x * (1.0 + lax.erf(x * (2.0 ** -0.5)))


def _peer_kernel(tie_ref, h_ref, u_ref, vt_ref, s1_ref, c1_ref, s2_ref, e2_ref, tau_ref,
                 idx_ref, gate_ref, o_ref, *, n_i, i_per_chunk, tok_chunk):
    t = pl.program_id(0)
    e = pl.program_id(1)
    width = h_ref.shape[0]

    @pl.when(e == 0)
    def _():
        o_ref[...] = jnp.zeros_like(o_ref)

    def gates_by_threshold(ii, tok):
        w = jnp.zeros((PEER_NKEYS, tok.stop - tok.start), F32)
        for h in range(PEER_HEADS):
            score = s2_ref[h, :, tok] + s1_ref[ii, h:h + 1, tok]
            w = w + (jnp.where(score >= tau_ref[h:h + 1, tok], e2_ref[h, :, tok], 0.0)
                     * c1_ref[ii, h:h + 1, tok])
        return w

    def gates_by_list(ii, tok):
        cols = tok.stop - tok.start
        ids = ((e * n_i + ii) * PEER_NKEYS
               + lax.broadcasted_iota(I32, (PEER_NKEYS, cols), 0))
        w = jnp.zeros((PEER_NKEYS, cols), F32)
        for h in range(PEER_HEADS):
            def slot(k, w):
                hit = ids == idx_ref[h, pl.ds(k, 1), tok]
                return w + jnp.where(hit, gate_ref[h, pl.ds(k, 1), tok], 0.0)
            w = lax.fori_loop(0, PEER_TOPK, slot, w)
        return w

    def step(gates):
        n_ec = n_i // i_per_chunk
        rows = i_per_chunk * PEER_NKEYS
        units = [(ec, tc) for tc in range(width // tok_chunk) for ec in range(n_ec)]

        def scores(ec, tc):
            return lax.dot_general(u_ref[ec * rows:(ec + 1) * rows, :],
                                   h_ref[tc * tok_chunk:(tc + 1) * tok_chunk, :],
                                   (((1,), (1,)), ((), ())), preferred_element_type=F32)

        def activations(s, ec, tc):
            tok = slice(tc * tok_chunk, (tc + 1) * tok_chunk)
            return jnp.concatenate(
                [(_gelu(s[ii * PEER_NKEYS:(ii + 1) * PEER_NKEYS])
                  * gates(ec * i_per_chunk + ii, tok)).astype(BF16)
                 for ii in range(i_per_chunk)], axis=0)

        def project(act, ec):
            return jnp.dot(vt_ref[0, :, ec * rows:(ec + 1) * rows], act,
                           preferred_element_type=F32)

        s_next = scores(*units[0])
        act_prev = None
        partial = None
        for c, (ec, tc) in enumerate(units):
            s_cur = s_next
            if c + 1 < len(units):
                s_next = scores(*units[c + 1])
            act = activations(s_cur, ec, tc)
            for done_act, (dec, dtc) in ([(act_prev, units[c - 1])] if c else []) + (
                    [(act, (ec, tc))] if c + 1 == len(units) else []):
                part = project(done_act, dec)
                partial = part if partial is None else partial + part
                if dec == n_ec - 1:
                    tok = slice(dtc * tok_chunk, (dtc + 1) * tok_chunk)
                    o_ref[:, tok] += partial
                    partial = None
            act_prev = act

    @pl.when(tie_ref[t] == 0)
    def _():
        step(gates_by_threshold)

    @pl.when(tie_ref[t] != 0)
    def _():
        step(gates_by_list)


def _peer(tile_ties, h2, u_bf, vt_tiles, s1, c1, s2, e2, tau, idx, gate):
    n_tok, d = h2.shape
    n_blocks, _, te = vt_tiles.shape
    tt = PEER_TT
    n_i = te // PEER_NKEYS
    once = dict(pipeline_mode=pl.Buffered(1))
    kspec = pl.BlockSpec((PEER_HEADS, PEER_NKEYS, tt), lambda t, e, ties: (0, 0, t), **once)
    ispec = pl.BlockSpec((n_i, PEER_HEADS, tt), lambda t, e, ties: (e, 0, t))
    lspec = pl.BlockSpec((PEER_HEADS, PEER_TOPK, tt), lambda t, e, ties: (0, 0, t), **once)
    grid_spec = pltpu.PrefetchScalarGridSpec(
        num_scalar_prefetch=1,
        grid=(n_tok // tt, n_blocks),
        in_specs=[pl.BlockSpec((tt, d), lambda t, e, ties: (t, 0), **once),
                  pl.BlockSpec((te, d), lambda t, e, ties: (e, 0)),
                  pl.BlockSpec((1, d, te), lambda t, e, ties: (e, 0, 0)),
                  ispec, ispec, kspec, kspec,
                  pl.BlockSpec((PEER_HEADS, tt), lambda t, e, ties: (0, t), **once),
                  lspec, lspec],
        out_specs=pl.BlockSpec((d, tt), lambda t, e, ties: (0, t), **once),
    )
    return pl.pallas_call(
        functools.partial(_peer_kernel, n_i=n_i, i_per_chunk=min(n_i, PEER_CHUNK_KEYS),
                          tok_chunk=min(tt, PEER_TOK_CHUNK)),
        out_shape=jax.ShapeDtypeStruct((d, n_tok), F32),
        grid_spec=grid_spec,
        compiler_params=_params(("parallel", "arbitrary")),
        name="peer_experts",
    )(tile_ties, h2, u_bf, vt_tiles, s1, c1, s2, e2, tau, idx, gate)


def _final_kernel(x_ref, yt_ref, gate_ref, g_ref, o_ref):
    x = x_ref[...] + gate_ref[0] * yt_ref[...].T
    o_ref[...] = x * lax.rsqrt(jnp.mean(x * x, axis=-1, keepdims=True) + EPS) * g_ref[...]


def _final(x1, y_t, gate, g_final, seq, ts=256):
    m, d = x1.shape
    return pl.pallas_call(
        _final_kernel,
        out_shape=jax.ShapeDtypeStruct((m, d), F32),
        grid=(m // ts,),
        in_specs=[pl.BlockSpec((ts, d), lambda i: (i, 0)),
                  pl.BlockSpec((d, ts), lambda i: (0, i)),
                  pl.BlockSpec((1, 1, d), lambda i: ((i * ts) // seq, 0, 0)),
                  pl.BlockSpec((1, d), lambda i: (0, 0))],
        out_specs=pl.BlockSpec((ts, d), lambda i: (i, 0)),
        compiler_params=_params(("parallel",)),
        name="final_norm",
    )(x1, y_t, gate, g_final.reshape(1, d))


def _layer(x, silu_mod, g_norm1, w_in, g_attn_head, w_pool, s_pool, w_out, g_norm2, w_query,
           sub_keys_1, sub_keys_2, u_experts, v_experts):
    b, s, d = x.shape
    n_tok = b * s
    n_heads = g_attn_head.shape[0]
    d_attn = n_heads * HEAD_DIM
    mod = silu_mod.reshape(b, N_MOD, 1, d)
    shift1, scale1, gate1, shift2, scale2, gate2 = (mod[:, i] for i in range(N_MOD))

    h = _norm_mod(x, g_norm1, scale1, shift1).reshape(n_tok, d)
    proj = _matmul(h, w_in.astype(BF16), tm=1024, tn=512, name="in_proj")
    o_attn = _attention(proj, g_attn_head, b, s, n_heads)
    o_pool = _pool(proj, w_pool.astype(BF16), s_pool, s, col_block=(3 * d_attn) // (d - d_attn))
    x1 = _out_proj(o_attn, o_pool, w_out.astype(BF16), x.reshape(n_tok, d), gate1, s)

    h2 = _norm_mod(x1.reshape(b, s, d), g_norm2, scale2, shift2).reshape(n_tok, d)
    q = _matmul(h2, w_query.astype(BF16), tm=1024, tn=512, name="peer_query")
    s1, c1, s2, e2, tau, tie, idx, gate = _retrieve(
        q, sub_keys_1.astype(BF16), sub_keys_2.astype(BF16))
    tile_ties = jnp.max(tie.reshape(PEER_HEADS, n_tok // PEER_TT, PEER_TT), axis=(0, 2))
    n_exp = v_experts.shape[0]
    vt_tiles = jnp.transpose(v_experts.reshape(n_exp // PEER_TE, PEER_TE, d),
                             (0, 2, 1)).astype(BF16)
    y_t = _peer(tile_ties, h2, u_experts.astype(BF16), vt_tiles,
                jnp.transpose(s1, (1, 0, 2)), jnp.transpose(c1, (1, 0, 2)), s2, e2, tau,
                idx, gate)
    return x1, y_t, gate2


def kernel(x, c, w_ada, b_ada, g_norm1, w_in, g_attn_head, w_pool, s_pool, w_out, g_norm2,
           w_query, sub_keys_1, sub_keys_2, u_experts, v_experts, g_final):
    b, s, d = x.shape
    depth = w_ada.shape[0]
    assert depth == 1, "the fused final norm assumes a single layer"
    mod = _ada_mod(c, w_ada[0], b_ada[0])
    x1, y_t, gate2 = _layer(x, mod, g_norm1[0], w_in[0], g_attn_head[0], w_pool[0], s_pool[0],
                            w_out[0], g_norm2[0], w_query[0], sub_keys_1[0], sub_keys_2[0],
                            u_experts[0], v_experts[0])
    out = _final(x1, y_t, gate2, g_final, s)
    return out.reshape(b, s, d)
```

```python
import functools

import jax
import jax.numpy as jnp
from jax import lax
from jax.experimental import pallas as pl
from jax.experimental.pallas import tpu as pltpu

F32 = jnp.float32
BF16 = jnp.bfloat16
I32 = jnp.int32

HEAD_DIM = 128
POOL_WINDOWS = (2, 4, 8, 16)
POOL_HALO = 16
PEER_HEADS = 8
PEER_TOPK = 16
PEER_NKEYS = 128
PEER_CHUNK_KEYS = 2
PEER_TOK_CHUNK = 256
PEER_TT = 512
PEER_TE = 512
N_MOD = 6
EPS = 1e-6
MIN_SUBLANES = 8
VMEM_LIMIT = 56 * 1024 * 1024


def _params(semantics, vmem=VMEM_LIMIT):
    return pltpu.CompilerParams(dimension_semantics=semantics, vmem_limit_bytes=vmem)


def _ada_kernel(c_ref, w_ref, b_ref, o_ref):
    c = c_ref[...]
    silu = c / (1.0 + jnp.exp(-c))
    o_ref[...] = jnp.dot(silu.astype(BF16), w_ref[...].astype(BF16),
                         preferred_element_type=F32) + b_ref[...]


def _ada_mod(c, w_ada, b_ada, tn=512):
    b, d = c.shape
    n = w_ada.shape[1]
    rows = -(-b // MIN_SUBLANES) * MIN_SUBLANES
    c_pad = jnp.pad(c, ((0, rows - b), (0, 0)))
    out = pl.pallas_call(
        _ada_kernel,
        out_shape=jax.ShapeDtypeStruct((rows, n), F32),
        grid=(n // tn,),
        in_specs=[pl.BlockSpec((rows, d), lambda j: (0, 0)),
                  pl.BlockSpec((d, tn), lambda j: (0, j)),
                  pl.BlockSpec((1, tn), lambda j: (0, j))],
        out_specs=pl.BlockSpec((rows, tn), lambda j: (0, j)),
        compiler_params=_params(("arbitrary",)),
        name="ada_mod",
    )(c_pad, w_ada, b_ada.reshape(1, n))
    return out[:b]


def _norm_mod_kernel(x_ref, g_ref, sc_ref, sh_ref, o_ref):
    x = x_ref[0]
    y = x * lax.rsqrt(jnp.mean(x * x, axis=-1, keepdims=True) + EPS) * g_ref[...]
    o_ref[0] = (y * (1.0 + sc_ref[0]) + sh_ref[0]).astype(o_ref.dtype)


def _norm_mod(x, g, scale, shift, ts=512):
    b, s, d = x.shape
    return pl.pallas_call(
        _norm_mod_kernel,
        out_shape=jax.ShapeDtypeStruct((b, s, d), BF16),
        grid=(b, s // ts),
        in_specs=[pl.BlockSpec((1, ts, d), lambda i, j: (i, j, 0)),
                  pl.BlockSpec((1, d), lambda i, j: (0, 0)),
                  pl.BlockSpec((1, 1, d), lambda i, j: (i, 0, 0)),
                  pl.BlockSpec((1, 1, d), lambda i, j: (i, 0, 0))],
        out_specs=pl.BlockSpec((1, ts, d), lambda i, j: (i, j, 0)),
        compiler_params=_params(("parallel", "parallel")),
        name="norm_mod",
    )(x, g.reshape(1, d), scale, shift)


def _mm_kernel(a_ref, w_ref, o_ref):
    o_ref[...] = jnp.dot(a_ref[...], w_ref[...],
                         preferred_element_type=F32).astype(o_ref.dtype)


def _matmul(a, w, tm, tn, out_dtype=BF16, name="matmul"):
    m, k = a.shape
    n = w.shape[1]
    return pl.pallas_call(
        _mm_kernel,
        out_shape=jax.ShapeDtypeStruct((m, n), out_dtype),
        grid=(m // tm, n // tn),
        in_specs=[pl.BlockSpec((tm, k), lambda i, j: (i, 0)),
                  pl.BlockSpec((k, tn), lambda i, j: (0, j))],
        out_specs=pl.BlockSpec((tm, tn), lambda i, j: (i, j)),
        compiler_params=_params(("parallel", "arbitrary")),
        name=name,
    )(a, w)


_SIGN_BIT = 0x80000000
_BF16_BITS = 0xFFFF0000
_LOG2_E = 1.4426950408889634


def _attn_kernel(q_ref, k_ref, v_ref, g_ref, o_ref, vt_scr, nz_a, nz_b, lg_a, lg_b,
                 *, tq, tk, neg_scale2):
    seq = q_ref.shape[0]
    for j in range(seq // tk):
        vt_scr[j] = v_ref[j * tk:(j + 1) * tk, :].astype(F32).T.astype(BF16)
    row = lax.broadcasted_iota(I32, (tk, 2 * tk), 0)
    col = lax.broadcasted_iota(I32, (tk, 2 * tk), 1)
    col = jnp.where(col >= tk, col - tk, col)
    suffix2 = jnp.where(col >= row, 1.0, 0.0).astype(BF16)
    key_minus_query = (lax.broadcasted_iota(I32, (tk, tq), 0)
                       - lax.broadcasted_iota(I32, (tk, tq), 1))
    g = g_ref[0]

    def log_terms(q, kt, valid, nz_dst, lg_dst):
        kb = k_ref[pl.ds(pl.multiple_of(kt * tk, tk), tk), :]
        nz = lax.dot_general(kb, q, (((1,), (1,)), ((), ())),
                             preferred_element_type=F32) * neg_scale2
        neg_abs = lax.bitcast_convert_type(
            lax.bitcast_convert_type(nz, jnp.uint32) | jnp.uint32(_SIGN_BIT), F32)
        log_not = jnp.minimum(nz, 0.0) - jnp.log2(1.0 + jnp.exp2(neg_abs))
        if valid is not None:
            log_not = jnp.where(valid, log_not, 0.0)
        hi = lax.bitcast_convert_type(
            lax.bitcast_convert_type(log_not, jnp.uint32) & jnp.uint32(_BF16_BITS), F32)
        nz_dst[...] = nz
        lg_dst[0:tk, :] = hi.astype(BF16)
        lg_dst[tk:2 * tk, :] = (log_not - hi).astype(BF16)

    def weigh(kt, valid, nz_src, lg_src, run, acc):
        incl = jnp.dot(suffix2, lg_src[...], preferred_element_type=F32)
        a = jnp.exp2(incl + run - nz_src[...])
        if valid is not None:
            a = jnp.where(valid, a, 0.0)
        acc = acc + jnp.dot(vt_scr[kt], a.astype(BF16), preferred_element_type=F32)
        return run + incl[0:1, :], acc

    def q_block(qi, carry):
        q0 = pl.multiple_of(qi * tq, tq)
        q = q_ref[pl.ds(q0, tq), :]
        run = jnp.zeros((1, tq), F32)
        acc = jnp.zeros((HEAD_DIM, tq), F32)
        top = qi * 2
        upper = key_minus_query < -tk
        lower = key_minus_query < 0
        log_terms(q, top + 1, upper, nz_a, lg_a)
        log_terms(q, top, lower, nz_b, lg_b)
        run, acc = weigh(top + 1, upper, nz_a, lg_a, run, acc)
        log_terms(q, jnp.maximum(top - 1, 0), None, nz_a, lg_a)
        run, acc = weigh(top, lower, nz_b, lg_b, run, acc)

        def two_tiles(p, c):
            run, acc = c
            kt = top - 1 - 2 * p
            log_terms(q, kt - 1, None, nz_b, lg_b)
            run, acc = weigh(kt, None, nz_a, lg_a, run, acc)
            log_terms(q, jnp.maximum(kt - 2, 0), None, nz_a, lg_a)
            run, acc = weigh(kt - 1, None, nz_b, lg_b, run, acc)
            return run, acc

        run, acc = lax.fori_loop(0, qi, two_tiles, (run, acc))
        y = acc * lax.rsqrt(jnp.mean(acc * acc, axis=0, keepdims=True) + EPS) * g
        o_ref[pl.ds(q0, tq), :] = y.T.astype(o_ref.dtype)
        return carry

    assert tq == 2 * tk
    lax.fori_loop(0, seq // tq, q_block, 0)


def _attention(proj, g_head, batch, seq, n_heads, tq=512, tk=256):
    n_tok = proj.shape[0]
    d_attn = n_heads * HEAD_DIM
    return pl.pallas_call(
        functools.partial(_attn_kernel, tq=tq, tk=tk,
                          neg_scale2=-(HEAD_DIM ** -0.5) * _LOG2_E),
        out_shape=jax.ShapeDtypeStruct((n_tok, d_attn), BF16),
        grid=(batch, n_heads),
        in_specs=[pl.BlockSpec((seq, HEAD_DIM), lambda b, h: (b, h)),
                  pl.BlockSpec((seq, HEAD_DIM), lambda b, h: (b, n_heads + h)),
                  pl.BlockSpec((seq, HEAD_DIM), lambda b, h: (b, 2 * n_heads + h)),
                  pl.BlockSpec((1, HEAD_DIM, 1), lambda b, h: (h, 0, 0))],
        out_specs=pl.BlockSpec((seq, HEAD_DIM), lambda b, h: (b, h)),
        scratch_shapes=[pltpu.VMEM((seq // tk, HEAD_DIM, tk), BF16),
                        pltpu.VMEM((tk, tq), F32), pltpu.VMEM((tk, tq), F32),
                        pltpu.VMEM((2 * tk, tq), BF16), pltpu.VMEM((2 * tk, tq), BF16)],
        compiler_params=_params(("parallel", "parallel")),
        name="stickbreak_attn",
    )(proj, proj, proj, g_head.reshape(n_heads, HEAD_DIM, 1))


def _pool_kernel(p_ref, halo_ref, w_ref, s_ref, o_ref, *, ts, seq_tiles, group):
    tile = pl.program_id(0) % seq_tiles
    p = p_ref[...].astype(F32)
    halo = jnp.where(tile == 0, 0.0, halo_ref[...].astype(F32))
    ext = jnp.concatenate([halo, p], axis=0)
    pos = tile * ts + lax.broadcasted_iota(I32, (ts, 1), 0)
    for gi, w in enumerate(POOL_WINDOWS):
        cols = slice(gi * group, (gi + 1) * group)
        tot = ext[:, cols]
        span = 1
        while span < w:
            tot = tot + pltpu.roll(tot, span, 0)
            span *= 2
        count = jnp.minimum(pos + 1, w).astype(F32)
        d = tot[POOL_HALO:] / count - p[:, cols]
        y = jnp.dot(d.astype(BF16), w_ref[gi], preferred_element_type=F32) * s_ref[:, cols]
        o_ref[:, cols] = y.astype(o_ref.dtype)


def _pool(proj, w_pool, s_pool, seq, col_block, ts=512):
    n_tok = proj.shape[0]
    n_groups, group, _ = w_pool.shape
    d_pool = n_groups * group
    seq_tiles = seq // ts
    halo_per_tile = ts // POOL_HALO
    return pl.pallas_call(
        functools.partial(_pool_kernel, ts=ts, seq_tiles=seq_tiles, group=group),
        out_shape=jax.ShapeDtypeStruct((n_tok, d_pool), BF16),
        grid=(n_tok // ts,),
        in_specs=[pl.BlockSpec((ts, d_pool), lambda i: (i, col_block)),
                  pl.BlockSpec((POOL_HALO, d_pool),
                               lambda i: (jnp.maximum(i * halo_per_tile - 1, 0), col_block)),
                  pl.BlockSpec((n_groups, group, group), lambda i: (0, 0, 0)),
                  pl.BlockSpec((1, d_pool), lambda i: (0, 0))],
        out_specs=pl.BlockSpec((ts, d_pool), lambda i: (i, 0)),
        compiler_params=_params(("parallel",)),
        name="multiscale_pool",
    )(proj, proj, w_pool, s_pool.reshape(1, d_pool))


def _out_proj_kernel(a1_ref, a2_ref, w1_ref, w2_ref, x_ref, gate_ref, o_ref):
    mix = (jnp.dot(a1_ref[...], w1_ref[...], preferred_element_type=F32)
           + jnp.dot(a2_ref[...], w2_ref[...], preferred_element_type=F32))
    o_ref[...] = x_ref[...] + gate_ref[0] * mix


def _out_proj(o_attn, o_pool, w_out, x2d, gate, seq, tm=512, tn=1024):
    m, d = x2d.shape
    k1, k2 = o_attn.shape[1], o_pool.shape[1]
    assert k1 == k2
    return pl.pallas_call(
        _out_proj_kernel,
        out_shape=jax.ShapeDtypeStruct((m, d), F32),
        grid=(m // tm, d // tn),
        in_specs=[pl.BlockSpec((tm, k1), lambda i, j: (i, 0)),
                  pl.BlockSpec((tm, k2), lambda i, j: (i, 0)),
                  pl.BlockSpec((k1, tn), lambda i, j: (0, j)),
                  pl.BlockSpec((k2, tn), lambda i, j: (1, j)),
                  pl.BlockSpec((tm, tn), lambda i, j: (i, j)),
                  pl.BlockSpec((1, 1, tn), lambda i, j: ((i * tm) // seq, 0, j))],
        out_specs=pl.BlockSpec((tm, tn), lambda i, j: (i, j)),
        compiler_params=_params(("parallel", "arbitrary")),
        name="out_proj",
    )(o_attn, o_pool, w_out, w_out, x2d, gate)


def _top_rows(vals, k):
    n_rows, width = vals.shape
    rows = lax.broadcasted_iota(I32, (n_rows, width), 0)
    slot = lax.broadcasted_iota(I32, (k, width), 0)
    out_v = jnp.zeros((k, width), F32)
    out_i = jnp.zeros((k, width), I32)
    for r in range(k):
        m = jnp.max(vals, axis=0, keepdims=True)
        idx = jnp.min(jnp.where(vals == m, rows, n_rows), axis=0, keepdims=True)
        out_v = jnp.where(slot == r, m, out_v)
        out_i = jnp.where(slot == r, idx, out_i)
        vals = jnp.where(rows == idx, -jnp.inf, vals)
    return out_v, out_i, vals


def _pick_rows(table, sel):
    out = jnp.zeros(sel.shape, table.dtype)
    for r in range(table.shape[0]):
        out = jnp.where(sel == r, table[r:r + 1], out)
    return out


def _staircase(v1, v2):
    width = v1.shape[1]
    r8 = lax.broadcasted_iota(I32, (MIN_SUBLANES, width), 0)
    ninf = jnp.float32(-jnp.inf)
    lead = v2[0:MIN_SUBLANES]
    groups = [
        v1[0:1] + v2,
        v1[1:2] + lead,
        jnp.where(r8 < 5, v1[2:3] + lead, ninf),
        jnp.where(r8 < 4, v1[3:4] + lead,
                  jnp.where(r8 < 7, v1[4:5] + pltpu.roll(lead, 4, 0), ninf)),
        jnp.where(r8 < 6,
                  jnp.where(r8 < 2, v1[5:6], jnp.where(r8 < 4, v1[6:7], v1[7:8]))
                  + jnp.where(r8 % 2 == 0, v2[0:1], v2[1:2]), ninf),
        v1[MIN_SUBLANES:] + v2[0:1],
    ]
    return jnp.concatenate(groups, axis=0)


def _staircase_coords(row):
    a = jnp.where(row < 16, 0, jnp.where(row < 24, 1, jnp.where(row < 32, 2, jnp.where(
        row < 36, 3, jnp.where(row < 40, 4, jnp.where(row < 48, 5 + (row - 40) // 2, row - 40))))))
    b = jnp.where(row < 16, row, jnp.where(row < 32, row % 8, jnp.where(
        row < 40, row % 4, jnp.where(row < 48, row % 2, 0))))
    return a, b


def _retrieve_kernel(q_ref, k1_ref, k2_ref,
                     s1_ref, c1_ref, s2_ref, e2_ref, tau_ref, tie_ref, idx_ref, gate_ref):
    half = PEER_NKEYS
    nt = (((1,), (1,)), ((), ()))
    for h in range(PEER_HEADS):
        q1 = q_ref[:, (2 * h) * half:(2 * h + 1) * half]
        q2 = q_ref[:, (2 * h + 1) * half:(2 * h + 2) * half]
        s1 = lax.dot_general(k1_ref[...], q1, nt, preferred_element_type=F32)
        s2 = lax.dot_general(k2_ref[...], q2, nt, preferred_element_type=F32)
        v1, i1, rest1 = _top_rows(s1, PEER_TOPK)
        v2, i2, rest2 = _top_rows(s2, PEER_TOPK)
        top_s, pos, rest_c = _top_rows(_staircase(v1, v2), PEER_TOPK)
        tau = top_s[PEER_TOPK - 1:PEER_TOPK]
        z = jnp.exp(top_s - top_s[0:1])
        denom = jnp.sum(z, axis=0, keepdims=True)
        next1 = jnp.max(rest1, axis=0, keepdims=True)
        next2 = jnp.max(rest2, axis=0, keepdims=True)
        next_c = jnp.max(rest_c, axis=0, keepdims=True)
        tie = (next_c >= tau) | (next1 + v2[0:1] >= tau) | (v1[0:1] + next2 >= tau)

        s1_ref[h] = s1
        s2_ref[h] = s2
        c1_ref[h] = jnp.exp(s1 - v1[0:1])
        e2_ref[h] = jnp.exp(s2 - v2[0:1]) / denom
        tau_ref[h:h + 1, :] = tau
        tie_ref[h:h + 1, :] = jnp.where(tie, 1, 0).astype(I32)
        a, b = _staircase_coords(pos)
        idx_ref[h] = _pick_rows(i1, a) * PEER_NKEYS + _pick_rows(i2, b)
        gate_ref[h] = z / denom


def _retrieve(q, keys1, keys2, tt=256):
    n_tok, qw = q.shape
    kshape = (PEER_HEADS, PEER_NKEYS, n_tok)
    lshape = (PEER_HEADS, PEER_TOPK, n_tok)
    kspec = pl.BlockSpec((PEER_HEADS, PEER_NKEYS, tt), lambda i: (0, 0, i))
    lspec = pl.BlockSpec((PEER_HEADS, PEER_TOPK, tt), lambda i: (0, 0, i))
    hspec = pl.BlockSpec((PEER_HEADS, tt), lambda i: (0, i))
    return pl.pallas_call(
        _retrieve_kernel,
        out_shape=(jax.ShapeDtypeStruct(kshape, F32), jax.ShapeDtypeStruct(kshape, F32),
                   jax.ShapeDtypeStruct(kshape, F32), jax.ShapeDtypeStruct(kshape, F32),
                   jax.ShapeDtypeStruct((PEER_HEADS, n_tok), F32),
                   jax.ShapeDtypeStruct((PEER_HEADS, n_tok), I32),
                   jax.ShapeDtypeStruct(lshape, I32), jax.ShapeDtypeStruct(lshape, F32)),
        grid=(n_tok // tt,),
        in_specs=[pl.BlockSpec((tt, qw), lambda i: (i, 0)),
                  pl.BlockSpec((PEER_NKEYS, PEER_NKEYS), lambda i: (0, 0)),
                  pl.BlockSpec((PEER_NKEYS, PEER_NKEYS), lambda i: (0, 0))],
        out_specs=(kspec, kspec, kspec, kspec, hspec, hspec, lspec, lspec),
        compiler_params=_params(("parallel",)),
        name="peer_retrieve",
    )(q, keys1, keys2)


def _gelu(x):
    return 0.5 * x * (1.0 + lax.erf(x * (2.0 ** -0.5)))


def _peer_kernel(tie_ref, h_ref, u_ref, vt_ref, s1_ref, c1_ref, s2_ref, e2_ref, tau_ref,
                 idx_ref, gate_ref, o_ref, *, n_i, i_per_chunk, tok_chunk):
    t = pl.program_id(0)
    e = pl.program_id(1)
    width = h_ref.shape[0]

    @pl.when(e == 0)
    def _():
        o_ref[...] = jnp.zeros_like(o_ref)

    def gates_by_threshold(ii, tok):
        w = jnp.zeros((PEER_NKEYS, tok.stop - tok.start), F32)
        for h in range(PEER_HEADS):
            score = s2_ref[h, :, tok] + s1_ref[ii, h:h + 1, tok]
            w = w + (jnp.where(score >= tau_ref[h:h + 1, tok], e2_ref[h, :, tok], 0.0)
                     * c1_ref[ii, h:h + 1, tok])
        return w

    def gates_by_list(ii, tok):
        cols = tok.stop - tok.start
        ids = ((e * n_i + ii) * PEER_NKEYS
               + lax.broadcasted_iota(I32, (PEER_NKEYS, cols), 0))
        w = jnp.zeros((PEER_NKEYS, cols), F32)
        for h in range(PEER_HEADS):
            def slot(k, w):
                hit = ids == idx_ref[h, pl.ds(k, 1), tok]
                return w + jnp.where(hit, gate_ref[h, pl.ds(k, 1), tok], 0.0)
            w = lax.fori_loop(0, PEER_TOPK, slot, w)
        return w

    def step(gates):
        n_ec = n_i // i_per_chunk
        rows = i_per_chunk * PEER_NKEYS
        units = [(ec, tc) for tc in range(width // tok_chunk) for ec in range(n_ec)]

        def scores(ec, tc):
            return lax.dot_general(u_ref[ec * rows:(ec + 1) * rows, :],
                                   h_ref[tc * tok_chunk:(tc + 1) * tok_chunk, :],
                                   (((1,), (1,)), ((), ())), preferred_element_type=F32)

        def activations(s, ec, tc):
            tok = slice(tc * tok_chunk, (tc + 1) * tok_chunk)
            return jnp.concatenate(
                [(_gelu(s[ii * PEER_NKEYS:(ii + 1) * PEER_NKEYS])
                  * gates(ec * i_per_chunk + ii, tok)).astype(BF16)
                 for ii in range(i_per_chunk)], axis=0)

        def project(act, ec):
            return jnp.dot(vt_ref[0, :, ec * rows:(ec + 1) * rows], act,
                           preferred_element_type=F32)

        s_next = scores(*units[0])
        act_prev = None
        partial = None
        for c, (ec, tc) in enumerate(units):
            s_cur = s_next
            if c + 1 < len(units):
                s_next = scores(*units[c + 1])
            act = activations(s_cur, ec, tc)
            for done_act, (dec, dtc) in ([(act_prev, units[c - 1])] if c else []) + (
                    [(act, (ec, tc))] if c + 1 == len(units) else []):
                part = project(done_act, dec)
                partial = part if partial is None else partial + part
                if dec == n_ec - 1:
                    tok = slice(dtc * tok_chunk, (dtc + 1) * tok_chunk)
                    o_ref[:, tok] += partial
                    partial = None
            act_prev = act

    @pl.when(tie_ref[t] == 0)
    def _():
        step(gates_by_threshold)

    @pl.when(tie_ref[t] != 0)
    def _():
        step(gates_by_list)


def _peer(tile_ties, h2, u_bf, vt_tiles, s1, c1, s2, e2, tau, idx, gate):
    n_tok, d = h2.shape
    n_blocks, _, te = vt_tiles.shape
    tt = PEER_TT
    n_i = te // PEER_NKEYS
    once = dict(pipeline_mode=pl.Buffered(1))
    kspec = pl.BlockSpec((PEER_HEADS, PEER_NKEYS, tt), lambda t, e, ties: (0, 0, t), **once)
    ispec = pl.BlockSpec((n_i, PEER_HEADS, tt), lambda t, e, ties: (e, 0, t))
    lspec = pl.BlockSpec((PEER_HEADS, PEER_TOPK, tt), lambda t, e, ties: (0, 0, t), **once)
    grid_spec = pltpu.PrefetchScalarGridSpec(
        num_scalar_prefetch=1,
        grid=(n_tok // tt, n_blocks),
        in_specs=[pl.BlockSpec((tt, d), lambda t, e, ties: (t, 0), **once),
                  pl.BlockSpec((te, d), lambda t, e, ties: (e, 0)),
                  pl.BlockSpec((1, d, te), lambda t, e, ties: (e, 0, 0)),
                  ispec, ispec, kspec, kspec,
                  pl.BlockSpec((PEER_HEADS, tt), lambda t, e, ties: (0, t), **once),
                  lspec, lspec],
        out_specs=pl.BlockSpec((d, tt), lambda t, e, ties: (0, t), **once),
    )
    return pl.pallas_call(
        functools.partial(_peer_kernel, n_i=n_i, i_per_chunk=min(n_i, PEER_CHUNK_KEYS),
                          tok_chunk=min(tt, PEER_TOK_CHUNK)),
        out_shape=jax.ShapeDtypeStruct((d, n_tok), F32),
        grid_spec=grid_spec,
        compiler_params=_params(("parallel", "arbitrary")),
        name="peer_experts",
    )(tile_ties, h2, u_bf, vt_tiles, s1, c1, s2, e2, tau, idx, gate)


def _final_kernel(x_ref, yt_ref, gate_ref, g_ref, o_ref):
    x = x_ref[...] + gate_ref[0] * yt_ref[...].T
    o_ref[...] = x * lax.rsqrt(jnp.mean(x * x, axis=-1, keepdims=True) + EPS) * g_ref[...]


def _final(x1, y_t, gate, g_final, seq, ts=256):
    m, d = x1.shape
    return pl.pallas_call(
        _final_kernel,
        out_shape=jax.ShapeDtypeStruct((m, d), F32),
        grid=(m // ts,),
        in_specs=[pl.BlockSpec((ts, d), lambda i: (i, 0)),
                  pl.BlockSpec((d, ts), lambda i: (0, i)),
                  pl.BlockSpec((1, 1, d), lambda i: ((i * ts) // seq, 0, 0)),
                  pl.BlockSpec((1, d), lambda i: (0, 0))],
        out_specs=pl.BlockSpec((ts, d), lambda i: (i, 0)),
        compiler_params=_params(("parallel",)),
        name="final_norm",
    )(x1, y_t, gate, g_final.reshape(1, d))


def _layer(x, silu_mod, g_norm1, w_in, g_attn_head, w_pool, s_pool, w_out, g_norm2, w_query,
           sub_keys_1, sub_keys_2, u_experts, v_experts):
    b, s, d = x.shape
    n_tok = b * s
    n_heads = g_attn_head.shape[0]
    d_attn = n_heads * HEAD_DIM
    mod = silu_mod.reshape(b, N_MOD, 1, d)
    shift1, scale1, gate1, shift2, scale2, gate2 = (mod[:, i] for i in range(N_MOD))

    h = _norm_mod(x, g_norm1, scale1, shift1).reshape(n_tok, d)
    proj = _matmul(h, w_in.astype(BF16), tm=1024, tn=512, name="in_proj")
    o_attn = _attention(proj, g_attn_head, b, s, n_heads)
    o_pool = _pool(proj, w_pool.astype(BF16), s_pool, s, col_block=(3 * d_attn) // (d - d_attn))
    x1 = _out_proj(o_attn, o_pool, w_out.astype(BF16), x.reshape(n_tok, d), gate1, s)

    h2 = _norm_mod(x1.reshape(b, s, d), g_norm2, scale2, shift2).reshape(n_tok, d)
    q = _matmul(h2, w_query.astype(BF16), tm=1024, tn=512, name="peer_query")
    s1, c1, s2, e2, tau, tie, idx, gate = _retrieve(
        q, sub_keys_1.astype(BF16), sub_keys_2.astype(BF16))
    tile_ties = jnp.max(tie.reshape(PEER_HEADS, n_tok // PEER_TT, PEER_TT), axis=(0, 2))
    n_exp = v_experts.shape[0]
    vt_tiles = jnp.transpose(v_experts.reshape(n_exp // PEER_TE, PEER_TE, d),
                             (0, 2, 1)).astype(BF16)
    y_t = _peer(tile_ties, h2, u_experts.astype(BF16), vt_tiles,
                jnp.transpose(s1, (1, 0, 2)), jnp.transpose(c1, (1, 0, 2)), s2, e2, tau,
                idx, gate)
    return x1, y_t, gate2


def kernel(x, c, w_ada, b_ada, g_norm1, w_in, g_attn_head, w_pool, s_pool, w_out, g_norm2,
           w_query, sub_keys_1, sub_keys_2, u_experts, v_experts, g_final):
    b, s, d = x.shape
    depth = w_ada.shape[0]
    assert depth == 1, "the fused final norm assumes a single layer"
    mod = _ada_mod(c, w_ada[0], b_ada[0])
    x1, y_t, gate2 = _layer(x, mod, g_norm1[0], w_in[0], g_attn_head[0], w_pool[0], s_pool[0],
                            w_out[0], g_norm2[0], w_query[0], sub_keys_1[0], sub_keys_2[0],
                            u_experts[0], v_experts[0])
    out = _final(x1, y_t, gate2, g_final, s)
    return out.reshape(b, s, d)
```

```python
import functools

import jax
import jax.numpy as jnp
from jax import lax
from jax.experimental import pallas as pl
from jax.experimental.pallas import tpu as pltpu

F32 = jnp.float32
BF16 = jnp.bfloat16
I32 = jnp.int32

HEAD_DIM = 128
POOL_WINDOWS = (2, 4, 8, 16)
POOL_HALO = 16
PEER_HEADS = 8
PEER_TOPK = 16
PEER_NKEYS = 128
PEER_CHUNK_KEYS = 2
PEER_TOK_CHUNK = 256
PEER_TT = 512
PEER_TE = 512
N_MOD = 6
EPS = 1e-6
MIN_SUBLANES = 8
VMEM_LIMIT = 56 * 1024 * 1024


def _params(semantics, vmem=VMEM_LIMIT):
    return pltpu.CompilerParams(dimension_semantics=semantics, vmem_limit_bytes=vmem)


def _ada_kernel(c_ref, w_ref, b_ref, o_ref):
    c = c_ref[...]
    silu = c / (1.0 + jnp.exp(-c))
    o_ref[...] = jnp.dot(silu.astype(BF16), w_ref[...].astype(BF16),
                         preferred_element_type=F32) + b_ref[...]


def _ada_mod(c, w_ada, b_ada, tn=512):
    b, d = c.shape
    n = w_ada.shape[1]
    rows = -(-b // MIN_SUBLANES) * MIN_SUBLANES
    c_pad = jnp.pad(c, ((0, rows - b), (0, 0)))
    out = pl.pallas_call(
        _ada_kernel,
        out_shape=jax.ShapeDtypeStruct((rows, n), F32),
        grid=(n // tn,),
        in_specs=[pl.BlockSpec((rows, d), lambda j: (0, 0)),
                  pl.BlockSpec((d, tn), lambda j: (0, j)),
                  pl.BlockSpec((1, tn), lambda j: (0, j))],
        out_specs=pl.BlockSpec((rows, tn), lambda j: (0, j)),
        compiler_params=_params(("arbitrary",)),
        name="ada_mod",
    )(c_pad, w_ada, b_ada.reshape(1, n))
    return out[:b]


def _norm_mod_kernel(x_ref, g_ref, sc_ref, sh_ref, o_ref):
    x = x_ref[0]
    y = x * lax.rsqrt(jnp.mean(x * x, axis=-1, keepdims=True) + EPS) * g_ref[...]
    o_ref[0] = (y * (1.0 + sc_ref[0]) + sh_ref[0]).astype(o_ref.dtype)


def _norm_mod(x, g, scale, shift, ts=512):
    b, s, d = x.shape
    return pl.pallas_call(
        _norm_mod_kernel,
        out_shape=jax.ShapeDtypeStruct((b, s, d), BF16),
        grid=(b, s // ts),
        in_specs=[pl.BlockSpec((1, ts, d), lambda i, j: (i, j, 0)),
                  pl.BlockSpec((1, d), lambda i, j: (0, 0)),
                  pl.BlockSpec((1, 1, d), lambda i, j: (i, 0, 0)),
                  pl.BlockSpec((1, 1, d), lambda i, j: (i, 0, 0))],
        out_specs=pl.BlockSpec((1, ts, d), lambda i, j: (i, j, 0)),
        compiler_params=_params(("parallel", "parallel")),
        name="norm_mod",
    )(x, g.reshape(1, d), scale, shift)


def _mm_kernel(a_ref, w_ref, o_ref):
    o_ref[...] = jnp.dot(a_ref[...], w_ref[...],
                         preferred_element_type=F32).astype(o_ref.dtype)


def _matmul(a, w, tm, tn, out_dtype=BF16, name="matmul"):
    m, k = a.shape
    n = w.shape[1]
    return pl.pallas_call(
        _mm_kernel,
        out_shape=jax.ShapeDtypeStruct((m, n), out_dtype),
        grid=(m // tm, n // tn),
        in_specs=[pl.BlockSpec((tm, k), lambda i, j: (i, 0)),
                  pl.BlockSpec((k, tn), lambda i, j: (0, j))],
        out_specs=pl.BlockSpec((tm, tn), lambda i, j: (i, j)),
        compiler_params=_params(("parallel", "arbitrary")),
        name=name,
    )(a, w)


_SIGN_BIT = 0x80000000
_LOG2_E = 1.4426950408889634


def _attn_kernel(q_ref, k_ref, v_ref, g_ref, o_ref, vt_scr, nz_a, nz_b, lg_a, lg_b,
                 *, tq, tk, neg_scale2):
    seq = q_ref.shape[0]
    for j in range(seq // tk):
        vt_scr[j] = v_ref[j * tk:(j + 1) * tk, :].astype(F32).T.astype(BF16)
    row = lax.broadcasted_iota(I32, (tk, tk), 0)
    col = lax.broadcasted_iota(I32, (tk, tk), 1)
    suffix = jnp.where(col >= row, 1.0, 0.0).astype(BF16)
    key_minus_query = (lax.broadcasted_iota(I32, (tk, tq), 0)
                       - lax.broadcasted_iota(I32, (tk, tq), 1))
    g = g_ref[0]

    def log_terms(q, kt, valid, nz_dst, lg_dst):
        kb = k_ref[pl.ds(pl.multiple_of(kt * tk, tk), tk), :]
        nz = lax.dot_general(kb, q, (((1,), (1,)), ((), ())),
                             preferred_element_type=F32) * neg_scale2
        neg_abs = lax.bitcast_convert_type(
            lax.bitcast_convert_type(nz, jnp.uint32) | jnp.uint32(_SIGN_BIT), F32)
        log_not = jnp.minimum(nz, 0.0) - jnp.log2(1.0 + jnp.exp2(neg_abs))
        if valid is not None:
            log_not = jnp.where(valid, log_not, 0.0)
        nz_dst[...] = nz
        lg_dst[...] = log_not.astype(BF16)

    def weigh(kt, valid, nz_src, lg_src, run, acc):
        incl = jnp.dot(suffix, lg_src[...], preferred_element_type=F32)
        a = jnp.exp2(incl + run - nz_src[...])
        if valid is not None:
            a = jnp.where(valid, a, 0.0)
        acc = acc + jnp.dot(vt_scr[kt], a.astype(BF16), preferred_element_type=F32)
        return run + incl[0:1, :], acc

    def q_block(qi, carry):
        q0 = pl.multiple_of(qi * tq, tq)
        q = q_ref[pl.ds(q0, tq), :]
        run = jnp.zeros((1, tq), F32)
        acc = jnp.zeros((HEAD_DIM, tq), F32)
        top = qi * 2
        upper = key_minus_query < -tk
        lower = key_minus_query < 0
        log_terms(q, top + 1, upper, nz_a, lg_a)
        log_terms(q, top, lower, nz_b, lg_b)
        run, acc = weigh(top + 1, upper, nz_a, lg_a, run, acc)
        log_terms(q, jnp.maximum(top - 1, 0), None, nz_a, lg_a)
        run, acc = weigh(top, lower, nz_b, lg_b, run, acc)

        def two_tiles(p, c):
            run, acc = c
            kt = top - 1 - 2 * p
            log_terms(q, kt - 1, None, nz_b, lg_b)
            run, acc = weigh(kt, None, nz_a, lg_a, run, acc)
            log_terms(q, jnp.maximum(kt - 2, 0), None, nz_a, lg_a)
            run, acc = weigh(kt - 1, None, nz_b, lg_b, run, acc)
            return run, acc

        run, acc = lax.fori_loop(0, qi, two_tiles, (run, acc))
        y = acc * lax.rsqrt(jnp.mean(acc * acc, axis=0, keepdims=True) + EPS) * g
        o_ref[pl.ds(q0, tq), :] = y.T.astype(o_ref.dtype)
        return carry

    assert tq == 2 * tk
    lax.fori_loop(0, seq // tq, q_block, 0)


def _attention(proj, g_head, batch, seq, n_heads, tq=512, tk=256):
    n_tok = proj.shape[0]
    d_attn = n_heads * HEAD_DIM
    return pl.pallas_call(
        functools.partial(_attn_kernel, tq=tq, tk=tk,
                          neg_scale2=-(HEAD_DIM ** -0.5) * _LOG2_E),
        out_shape=jax.ShapeDtypeStruct((n_tok, d_attn), BF16),
        grid=(batch, n_heads),
        in_specs=[pl.BlockSpec((seq, HEAD_DIM), lambda b, h: (b, h)),
                  pl.BlockSpec((seq, HEAD_DIM), lambda b, h: (b, n_heads + h)),
                  pl.BlockSpec((seq, HEAD_DIM), lambda b, h: (b, 2 * n_heads + h)),
                  pl.BlockSpec((1, HEAD_DIM, 1), lambda b, h: (h, 0, 0))],
        out_specs=pl.BlockSpec((seq, HEAD_DIM), lambda b, h: (b, h)),
        scratch_shapes=[pltpu.VMEM((seq // tk, HEAD_DIM, tk), BF16),
                        pltpu.VMEM((tk, tq), F32), pltpu.VMEM((tk, tq), F32),
                        pltpu.VMEM((tk, tq), BF16), pltpu.VMEM((tk, tq), BF16)],
        compiler_params=_params(("parallel", "parallel")),
        name="stickbreak_attn",
    )(proj, proj, proj, g_head.reshape(n_heads, HEAD_DIM, 1))


def _pool_kernel(p_ref, halo_ref, w_ref, s_ref, o_ref, *, ts, seq_tiles, group):
    tile = pl.program_id(0) % seq_tiles
    p = p_ref[...].astype(F32)
    halo = jnp.where(tile == 0, 0.0, halo_ref[...].astype(F32))
    ext = jnp.concatenate([halo, p], axis=0)
    pos = tile * ts + lax.broadcasted_iota(I32, (ts, 1), 0)
    for gi, w in enumerate(POOL_WINDOWS):
        cols = slice(gi * group, (gi + 1) * group)
        tot = ext[:, cols]
        span = 1
        while span < w:
            tot = tot + pltpu.roll(tot, span, 0)
            span *= 2
        count = jnp.minimum(pos + 1, w).astype(F32)
        d = tot[POOL_HALO:] / count - p[:, cols]
        y = jnp.dot(d.astype(BF16), w_ref[gi], preferred_element_type=F32) * s_ref[:, cols]
        o_ref[:, cols] = y.astype(o_ref.dtype)


def _pool(proj, w_pool, s_pool, seq, col_block, ts=512):
    n_tok = proj.shape[0]
    n_groups, group, _ = w_pool.shape
    d_pool = n_groups * group
    seq_tiles = seq // ts
    halo_per_tile = ts // POOL_HALO
    return pl.pallas_call(
        functools.partial(_pool_kernel, ts=ts, seq_tiles=seq_tiles, group=group),
        out_shape=jax.ShapeDtypeStruct((n_tok, d_pool), BF16),
        grid=(n_tok // ts,),
        in_specs=[pl.BlockSpec((ts, d_pool), lambda i: (i, col_block)),
                  pl.BlockSpec((POOL_HALO, d_pool),
                               lambda i: (jnp.maximum(i * halo_per_tile - 1, 0), col_block)),
                  pl.BlockSpec((n_groups, group, group), lambda i: (0, 0, 0)),
                  pl.BlockSpec((1, d_pool), lambda i: (0, 0))],
        out_specs=pl.BlockSpec((ts, d_pool), lambda i: (i, 0)),
        compiler_params=_params(("parallel",)),
        name="multiscale_pool",
    )(proj, proj, w_pool, s_pool.reshape(1, d_pool))


def _out_proj_kernel(a1_ref, a2_ref, w1_ref, w2_ref, x_ref, gate_ref, o_ref):
    mix = (jnp.dot(a1_ref[...], w1_ref[...], preferred_element_type=F32)
           + jnp.dot(a2_ref[...], w2_ref[...], preferred_element_type=F32))
    o_ref[...] = x_ref[...] + gate_ref[0] * mix


def _out_proj(o_attn, o_pool, w_out, x2d, gate, seq, tm=512, tn=1024):
    m, d = x2d.shape
    k1, k2 = o_attn.shape[1], o_pool.shape[1]
    assert k1 == k2
    return pl.pallas_call(
        _out_proj_kernel,
        out_shape=jax.ShapeDtypeStruct((m, d), F32),
        grid=(m // tm, d // tn),
        in_specs=[pl.BlockSpec((tm, k1), lambda i, j: (i, 0)),
                  pl.BlockSpec((tm, k2), lambda i, j: (i, 0)),
                  pl.BlockSpec((k1, tn), lambda i, j: (0, j)),
                  pl.BlockSpec((k2, tn), lambda i, j: (1, j)),
                  pl.BlockSpec((tm, tn), lambda i, j: (i, j)),
                  pl.BlockSpec((1, 1, tn), lambda i, j: ((i * tm) // seq, 0, j))],
        out_specs=pl.BlockSpec((tm, tn), lambda i, j: (i, j)),
        compiler_params=_params(("parallel", "arbitrary")),
        name="out_proj",
    )(o_attn, o_pool, w_out, w_out, x2d, gate)


def _top_rows(vals, k):
    n_rows, width = vals.shape
    rows = lax.broadcasted_iota(I32, (n_rows, width), 0)
    slot = lax.broadcasted_iota(I32, (k, width), 0)
    out_v = jnp.zeros((k, width), F32)
    out_i = jnp.zeros((k, width), I32)
    for r in range(k):
        m = jnp.max(vals, axis=0, keepdims=True)
        idx = jnp.min(jnp.where(vals == m, rows, n_rows), axis=0, keepdims=True)
        out_v = jnp.where(slot == r, m, out_v)
        out_i = jnp.where(slot == r, idx, out_i)
        vals = jnp.where(rows == idx, -jnp.inf, vals)
    return out_v, out_i, vals


def _pick_rows(table, sel):
    out = jnp.zeros(sel.shape, table.dtype)
    for r in range(table.shape[0]):
        out = jnp.where(sel == r, table[r:r + 1], out)
    return out


def _staircase(v1, v2):
    width = v1.shape[1]
    r8 = lax.broadcasted_iota(I32, (MIN_SUBLANES, width), 0)
    ninf = jnp.float32(-jnp.inf)
    lead = v2[0:MIN_SUBLANES]
    groups = [
        v1[0:1] + v2,
        v1[1:2] + lead,
        jnp.where(r8 < 5, v1[2:3] + lead, ninf),
        jnp.where(r8 < 4, v1[3:4] + lead,
                  jnp.where(r8 < 7, v1[4:5] + pltpu.roll(lead, 4, 0), ninf)),
        jnp.where(r8 < 6,
                  jnp.where(r8 < 2, v1[5:6], jnp.where(r8 < 4, v1[6:7], v1[7:8]))
                  + jnp.where(r8 % 2 == 0, v2[0:1], v2[1:2]), ninf),
        v1[MIN_SUBLANES:] + v2[0:1],
    ]
    return jnp.concatenate(groups, axis=0)


def _staircase_coords(row):
    a = jnp.where(row < 16, 0, jnp.where(row < 24, 1, jnp.where(row < 32, 2, jnp.where(
        row < 36, 3, jnp.where(row < 40, 4, jnp.where(row < 48, 5 + (row - 40) // 2, row - 40))))))
    b = jnp.where(row < 16, row, jnp.where(row < 32, row % 8, jnp.where(
        row < 40, row % 4, jnp.where(row < 48, row % 2, 0))))
    return a, b


def _retrieve_kernel(q_ref, k1_ref, k2_ref,
                     s1_ref, c1_ref, s2_ref, e2_ref, tau_ref, tie_ref, idx_ref, gate_ref):
    half = PEER_NKEYS
    nt = (((1,), (1,)), ((), ()))
    for h in range(PEER_HEADS):
        q1 = q_ref[:, (2 * h) * half:(2 * h + 1) * half]
        q2 = q_ref[:, (2 * h + 1) * half:(2 * h + 2) * half]
        s1 = lax.dot_general(k1_ref[...], q1, nt, preferred_element_type=F32)
        s2 = lax.dot_general(k2_ref[...], q2, nt, preferred_element_type=F32)
        v1, i1, rest1 = _top_rows(s1, PEER_TOPK)
        v2, i2, rest2 = _top_rows(s2, PEER_TOPK)
        top_s, pos, rest_c = _top_rows(_staircase(v1, v2), PEER_TOPK)
        tau = top_s[PEER_TOPK - 1:PEER_TOPK]
        z = jnp.exp(top_s - top_s[0:1])
        denom = jnp.sum(z, axis=0, keepdims=True)
        next1 = jnp.max(rest1, axis=0, keepdims=True)
        next2 = jnp.max(rest2, axis=0, keepdims=True)
        next_c = jnp.max(rest_c, axis=0, keepdims=True)
        tie = (next_c >= tau) | (next1 + v2[0:1] >= tau) | (v1[0:1] + next2 >= tau)

        s1_ref[h] = s1
        s2_ref[h] = s2
        c1_ref[h] = jnp.exp(s1 - v1[0:1])
        e2_ref[h] = jnp.exp(s2 - v2[0:1]) / denom
        tau_ref[h:h + 1, :] = tau
        tie_ref[h:h + 1, :] = jnp.where(tie, 1, 0).astype(I32)
        a, b = _staircase_coords(pos)
        idx_ref[h] = _pick_rows(i1, a) * PEER_NKEYS + _pick_rows(i2, b)
        gate_ref[h] = z / denom


def _retrieve(q, keys1, keys2, tt=256):
    n_tok, qw = q.shape
    kshape = (PEER_HEADS, PEER_NKEYS, n_tok)
    lshape = (PEER_HEADS, PEER_TOPK, n_tok)
    kspec = pl.BlockSpec((PEER_HEADS, PEER_NKEYS, tt), lambda i: (0, 0, i))
    lspec = pl.BlockSpec((PEER_HEADS, PEER_TOPK, tt), lambda i: (0, 0, i))
    hspec = pl.BlockSpec((PEER_HEADS, tt), lambda i: (0, i))
    return pl.pallas_call(
        _retrieve_kernel,
        out_shape=(jax.ShapeDtypeStruct(kshape, F32), jax.ShapeDtypeStruct(kshape, F32),
                   jax.ShapeDtypeStruct(kshape, F32), jax.ShapeDtypeStruct(kshape, F32),
                   jax.ShapeDtypeStruct((PEER_HEADS, n_tok), F32),
                   jax.ShapeDtypeStruct((PEER_HEADS, n_tok), I32),
                   jax.ShapeDtypeStruct(lshape, I32), jax.ShapeDtypeStruct(lshape, F32)),
        grid=(n_tok // tt,),
        in_specs=[pl.BlockSpec((tt, qw), lambda i: (i, 0)),
                  pl.BlockSpec((PEER_NKEYS, PEER_NKEYS), lambda i: (0, 0)),
                  pl.BlockSpec((PEER_NKEYS, PEER_NKEYS), lambda i: (0, 0))],
        out_specs=(kspec, kspec, kspec, kspec, hspec, hspec, lspec, lspec),
        compiler_params=_params(("parallel",)),
        name="peer_retrieve",
    )(q, keys1, keys2)


def _gelu(x):
    return 0.5 * x * (1.0 + lax.erf(x * (2.0 ** -0.5)))


def _peer_kernel(tie_ref, h_ref, u_ref, vt_ref, s1_ref, c1_ref, s2_ref, e2_ref, tau_ref,
                 idx_ref, gate_ref, o_ref, *, n_i, i_per_chunk, tok_chunk):
    t = pl.program_id(0)
    e = pl.program_id(1)
    width = h_ref.shape[0]

    @pl.when(e == 0)
    def _():
        o_ref[...] = jnp.zeros_like(o_ref)

    def gates_by_threshold(ii, tok):
        w = jnp.zeros((PEER_NKEYS, tok.stop - tok.start), F32)
        for h in range(PEER_HEADS):
            key = pl.ds(e * n_i + ii, 1)
            score = s2_ref[h, :, tok] + s1_ref[h, key, tok]
            w = w + (jnp.where(score >= tau_ref[h:h + 1, tok], e2_ref[h, :, tok], 0.0)
                     * c1_ref[h, key, tok])
        return w

    def gates_by_list(ii, tok):
        cols = tok.stop - tok.start
        ids = ((e * n_i + ii) * PEER_NKEYS
               + lax.broadcasted_iota(I32, (PEER_NKEYS, cols), 0))
        w = jnp.zeros((PEER_NKEYS, cols), F32)
        for h in range(PEER_HEADS):
            def slot(k, w):
                hit = ids == idx_ref[h, pl.ds(k, 1), tok]
                return w + jnp.where(hit, gate_ref[h, pl.ds(k, 1), tok], 0.0)
            w = lax.fori_loop(0, PEER_TOPK, slot, w)
        return w

    def step(gates):
        n_ec = n_i // i_per_chunk
        rows = i_per_chunk * PEER_NKEYS
        units = [(ec, tc) for tc in range(width // tok_chunk) for ec in range(n_ec)]

        def scores(ec, tc):
            return lax.dot_general(u_ref[ec * rows:(ec + 1) * rows, :],
                                   h_ref[tc * tok_chunk:(tc + 1) * tok_chunk, :],
                                   (((1,), (1,)), ((), ())), preferred_element_type=F32)

        def activations(s, ec, tc):
            tok = slice(tc * tok_chunk, (tc + 1) * tok_chunk)
            return jnp.concatenate(
                [(_gelu(s[ii * PEER_NKEYS:(ii + 1) * PEER_NKEYS])
                  * gates(ec * i_per_chunk + ii, tok)).astype(BF16)
                 for ii in range(i_per_chunk)], axis=0)

        def project(act, ec):
            return jnp.dot(vt_ref[0, :, ec * rows:(ec + 1) * rows], act,
                           preferred_element_type=F32)

        s_next = scores(*units[0])
        act_prev = None
        partial = None
        for c, (ec, tc) in enumerate(units):
            s_cur = s_next
            if c + 1 < len(units):
                s_next = scores(*units[c + 1])
            act = activations(s_cur, ec, tc)
            for done_act, (dec, dtc) in ([(act_prev, units[c - 1])] if c else []) + (
                    [(act, (ec, tc))] if c + 1 == len(units) else []):
                part = project(done_act, dec)
                partial = part if partial is None else partial + part
                if dec == n_ec - 1:
                    tok = slice(dtc * tok_chunk, (dtc + 1) * tok_chunk)
                    o_ref[:, tok] += partial
                    partial = None
            act_prev = act

    @pl.when(tie_ref[t] == 0)
    def _():
        step(gates_by_threshold)

    @pl.when(tie_ref[t] != 0)
    def _():
        step(gates_by_list)


def _peer(tile_ties, h2, u_bf, vt_tiles, s1, c1, s2, e2, tau, idx, gate):
    n_tok, d = h2.shape
    n_blocks, _, te = vt_tiles.shape
    tt = PEER_TT
    n_i = te // PEER_NKEYS
    once = dict(pipeline_mode=pl.Buffered(1))
    kspec = pl.BlockSpec((PEER_HEADS, PEER_NKEYS, tt), lambda t, e, ties: (0, 0, t), **once)
    lspec = pl.BlockSpec((PEER_HEADS, PEER_TOPK, tt), lambda t, e, ties: (0, 0, t), **once)
    grid_spec = pltpu.PrefetchScalarGridSpec(
        num_scalar_prefetch=1,
        grid=(n_tok // tt, n_blocks),
        in_specs=[pl.BlockSpec((tt, d), lambda t, e, ties: (t, 0), **once),
                  pl.BlockSpec((te, d), lambda t, e, ties: (e, 0)),
                  pl.BlockSpec((1, d, te), lambda t, e, ties: (e, 0, 0)),
                  kspec, kspec, kspec, kspec,
                  pl.BlockSpec((PEER_HEADS, tt), lambda t, e, ties: (0, t), **once),
                  lspec, lspec],
        out_specs=pl.BlockSpec((d, tt), lambda t, e, ties: (0, t), **once),
    )
    return pl.pallas_call(
        functools.partial(_peer_kernel, n_i=n_i, i_per_chunk=min(n_i, PEER_CHUNK_KEYS),
                          tok_chunk=min(tt, PEER_TOK_CHUNK)),
        out_shape=jax.ShapeDtypeStruct((d, n_tok), F32),
        grid_spec=grid_spec,
        compiler_params=_params(("parallel", "arbitrary")),
        name="peer_experts",
    )(tile_ties, h2, u_bf, vt_tiles, s1, c1, s2, e2, tau, idx, gate)


def _final_kernel(x_ref, yt_ref, gate_ref, g_ref, o_ref):
    x = x_ref[...] + gate_ref[0] * yt_ref[...].T
    o_ref[...] = x * lax.rsqrt(jnp.mean(x * x, axis=-1, keepdims=True) + EPS) * g_ref[...]


def _final(x1, y_t, gate, g_final, seq, ts=256):
    m, d = x1.shape
    return pl.pallas_call(
        _final_kernel,
        out_shape=jax.ShapeDtypeStruct((m, d), F32),
        grid=(m // ts,),
        in_specs=[pl.BlockSpec((ts, d), lambda i: (i, 0)),
                  pl.BlockSpec((d, ts), lambda i: (0, i)),
                  pl.BlockSpec((1, 1, d), lambda i: ((i * ts) // seq, 0, 0)),
                  pl.BlockSpec((1, d), lambda i: (0, 0))],
        out_specs=pl.BlockSpec((ts, d), lambda i: (i, 0)),
        compiler_params=_params(("parallel",)),
        name="final_norm",
    )(x1, y_t, gate, g_final.reshape(1, d))


def _layer(x, silu_mod, g_norm1, w_in, g_attn_head, w_pool, s_pool, w_out, g_norm2, w_query,
           sub_keys_1, sub_keys_2, u_experts, v_experts):
    b, s, d = x.shape
    n_tok = b * s
    n_heads = g_attn_head.shape[0]
    d_attn = n_heads * HEAD_DIM
    mod = silu_mod.reshape(b, N_MOD, 1, d)
    shift1, scale1, gate1, shift2, scale2, gate2 = (mod[:, i] for i in range(N_MOD))

    h = _norm_mod(x, g_norm1, scale1, shift1).reshape(n_tok, d)
    proj = _matmul(h, w_in.astype(BF16), tm=1024, tn=512, name="in_proj")
    o_attn = _attention(proj, g_attn_head, b, s, n_heads)
    o_pool = _pool(proj, w_pool.astype(BF16), s_pool, s, col_block=(3 * d_attn) // (d - d_attn))
    x1 = _out_proj(o_attn, o_pool, w_out.astype(BF16), x.reshape(n_tok, d), gate1, s)

    h2 = _norm_mod(x1.reshape(b, s, d), g_norm2, scale2, shift2).reshape(n_tok, d)
    q = _matmul(h2, w_query.astype(BF16), tm=1024, tn=512, name="peer_query")
    s1, c1, s2, e2, tau, tie, idx, gate = _retrieve(
        q, sub_keys_1.astype(BF16), sub_keys_2.astype(BF16))
    tile_ties = jnp.max(tie.reshape(PEER_HEADS, n_tok // PEER_TT, PEER_TT), axis=(0, 2))
    n_exp = v_experts.shape[0]
    vt_tiles = jnp.transpose(v_experts.reshape(n_exp // PEER_TE, PEER_TE, d),
                             (0, 2, 1)).astype(BF16)
    y_t = _peer(tile_ties, h2, u_experts.astype(BF16), vt_tiles, s1, c1, s2, e2, tau, idx, gate)
    return x1, y_t, gate2


def kernel(x, c, w_ada, b_ada, g_norm1, w_in, g_attn_head, w_pool, s_pool, w_out, g_norm2,
           w_query, sub_keys_1, sub_keys_2, u_experts, v_experts, g_final):
    b, s, d = x.shape
    depth = w_ada.shape[0]
    assert depth == 1, "the fused final norm assumes a single layer"
    mod = _ada_mod(c, w_ada[0], b_ada[0])
    x1, y_t, gate2 = _layer(x, mod, g_norm1[0], w_in[0], g_attn_head[0], w_pool[0], s_pool[0],
                            w_out[0], g_norm2[0], w_query[0], sub_keys_1[0], sub_keys_2[0],
                            u_experts[0], v_experts[0])
    out = _final(x1, y_t, gate2, g_final, s)
    return out.reshape(b, s, d)
```

```python
import functools

import jax
import jax.numpy as jnp
from jax import lax
from jax.experimental import pallas as pl
from jax.experimental.pallas import tpu as pltpu

F32 = jnp.float32
BF16 = jnp.bfloat16
I32 = jnp.int32

HEAD_DIM = 128
POOL_WINDOWS = (2, 4, 8, 16)
POOL_HALO = 16
PEER_HEADS = 8
PEER_TOPK = 16
PEER_NKEYS = 128
PEER_CHUNK_KEYS = 2
PEER_TOK_CHUNK = 256
PEER_TT = 512
PEER_TE = 512
N_MOD = 6
EPS = 1e-6
MIN_SUBLANES = 8
VMEM_LIMIT = 56 * 1024 * 1024


def _params(semantics, vmem=VMEM_LIMIT):
    return pltpu.CompilerParams(dimension_semantics=semantics, vmem_limit_bytes=vmem)


def _ada_kernel(c_ref, w_ref, b_ref, o_ref):
    c = c_ref[...]
    silu = c / (1.0 + jnp.exp(-c))
    o_ref[...] = jnp.dot(silu.astype(BF16), w_ref[...].astype(BF16),
                         preferred_element_type=F32) + b_ref[...]


def _ada_mod(c, w_ada, b_ada, tn=512):
    b, d = c.shape
    n = w_ada.shape[1]
    rows = -(-b // MIN_SUBLANES) * MIN_SUBLANES
    c_pad = jnp.pad(c, ((0, rows - b), (0, 0)))
    out = pl.pallas_call(
        _ada_kernel,
        out_shape=jax.ShapeDtypeStruct((rows, n), F32),
        grid=(n // tn,),
        in_specs=[pl.BlockSpec((rows, d), lambda j: (0, 0)),
                  pl.BlockSpec((d, tn), lambda j: (0, j)),
                  pl.BlockSpec((1, tn), lambda j: (0, j))],
        out_specs=pl.BlockSpec((rows, tn), lambda j: (0, j)),
        compiler_params=_params(("arbitrary",)),
        name="ada_mod",
    )(c_pad, w_ada, b_ada.reshape(1, n))
    return out[:b]


def _norm_mod_kernel(x_ref, g_ref, sc_ref, sh_ref, o_ref):
    x = x_ref[0]
    y = x * lax.rsqrt(jnp.mean(x * x, axis=-1, keepdims=True) + EPS) * g_ref[...]
    o_ref[0] = (y * (1.0 + sc_ref[0]) + sh_ref[0]).astype(o_ref.dtype)


def _norm_mod(x, g, scale, shift, ts=512):
    b, s, d = x.shape
    return pl.pallas_call(
        _norm_mod_kernel,
        out_shape=jax.ShapeDtypeStruct((b, s, d), BF16),
        grid=(b, s // ts),
        in_specs=[pl.BlockSpec((1, ts, d), lambda i, j: (i, j, 0)),
                  pl.BlockSpec((1, d), lambda i, j: (0, 0)),
                  pl.BlockSpec((1, 1, d), lambda i, j: (i, 0, 0)),
                  pl.BlockSpec((1, 1, d), lambda i, j: (i, 0, 0))],
        out_specs=pl.BlockSpec((1, ts, d), lambda i, j: (i, j, 0)),
        compiler_params=_params(("parallel", "parallel")),
        name="norm_mod",
    )(x, g.reshape(1, d), scale, shift)


def _mm_kernel(a_ref, w_ref, o_ref):
    o_ref[...] = jnp.dot(a_ref[...], w_ref[...],
                         preferred_element_type=F32).astype(o_ref.dtype)


def _matmul(a, w, tm, tn, out_dtype=BF16, name="matmul"):
    m, k = a.shape
    n = w.shape[1]
    return pl.pallas_call(
        _mm_kernel,
        out_shape=jax.ShapeDtypeStruct((m, n), out_dtype),
        grid=(m // tm, n // tn),
        in_specs=[pl.BlockSpec((tm, k), lambda i, j: (i, 0)),
                  pl.BlockSpec((k, tn), lambda i, j: (0, j))],
        out_specs=pl.BlockSpec((tm, tn), lambda i, j: (i, j)),
        compiler_params=_params(("parallel", "arbitrary")),
        name=name,
    )(a, w)


_SIGN_BIT = 0x80000000
_LOG2_E = 1.4426950408889634


def _attn_kernel(q_ref, k_ref, v_ref, g_ref, o_ref, vt_scr, nz_a, nz_b, lg_a, lg_b,
                 *, tq, tk, neg_scale2):
    seq = q_ref.shape[0]
    for j in range(seq // tk):
        vt_scr[j] = v_ref[j * tk:(j + 1) * tk, :].astype(F32).T.astype(BF16)
    row = lax.broadcasted_iota(I32, (tk, tk), 0)
    col = lax.broadcasted_iota(I32, (tk, tk), 1)
    suffix = jnp.where(col >= row, 1.0, 0.0).astype(BF16)
    key_minus_query = (lax.broadcasted_iota(I32, (tk, tq), 0)
                       - lax.broadcasted_iota(I32, (tk, tq), 1))
    g = g_ref[0]

    def log_terms(q, kt, valid, nz_dst, lg_dst):
        kb = k_ref[pl.ds(pl.multiple_of(kt * tk, tk), tk), :]
        nz = lax.dot_general(kb, q, (((1,), (1,)), ((), ())),
                             preferred_element_type=F32) * neg_scale2
        neg_abs = lax.bitcast_convert_type(
            lax.bitcast_convert_type(nz, jnp.uint32) | jnp.uint32(_SIGN_BIT), F32)
        log_not = jnp.minimum(nz, 0.0) - jnp.log2(1.0 + jnp.exp2(neg_abs))
        if valid is not None:
            log_not = jnp.where(valid, log_not, 0.0)
        nz_dst[...] = nz
        lg_dst[...] = log_not.astype(BF16)

    def weigh(kt, valid, nz_src, lg_src, run, acc):
        incl = jnp.dot(suffix, lg_src[...], preferred_element_type=F32)
        a = jnp.exp2(incl + run - nz_src[...])
        if valid is not None:
            a = jnp.where(valid, a, 0.0)
        acc = acc + jnp.dot(vt_scr[kt], a.astype(BF16), preferred_element_type=F32)
        return run + incl[0:1, :], acc

    def q_block(qi, carry):
        q0 = pl.multiple_of(qi * tq, tq)
        q = q_ref[pl.ds(q0, tq), :]
        run = jnp.zeros((1, tq), F32)
        acc = jnp.zeros((HEAD_DIM, tq), F32)
        top = qi * 2
        upper = key_minus_query < -tk
        lower = key_minus_query < 0
        log_terms(q, top + 1, upper, nz_a, lg_a)
        log_terms(q, top, lower, nz_b, lg_b)
        run, acc = weigh(top + 1, upper, nz_a, lg_a, run, acc)
        log_terms(q, jnp.maximum(top - 1, 0), None, nz_a, lg_a)
        run, acc = weigh(top, lower, nz_b, lg_b, run, acc)

        def two_tiles(p, c):
            run, acc = c
            kt = top - 1 - 2 * p
            log_terms(q, kt - 1, None, nz_b, lg_b)
            run, acc = weigh(kt, None, nz_a, lg_a, run, acc)
            log_terms(q, jnp.maximum(kt - 2, 0), None, nz_a, lg_a)
            run, acc = weigh(kt - 1, None, nz_b, lg_b, run, acc)
            return run, acc

        run, acc = lax.fori_loop(0, qi, two_tiles, (run, acc))
        y = acc * lax.rsqrt(jnp.mean(acc * acc, axis=0, keepdims=True) + EPS) * g
        o_ref[pl.ds(q0, tq), :] = y.T.astype(o_ref.dtype)
        return carry

    assert tq == 2 * tk
    lax.fori_loop(0, seq // tq, q_block, 0)


def _attention(proj, g_head, batch, seq, n_heads, tq=512, tk=256):
    n_tok = proj.shape[0]
    d_attn = n_heads * HEAD_DIM
    return pl.pallas_call(
        functools.partial(_attn_kernel, tq=tq, tk=tk,
                          neg_scale2=-(HEAD_DIM ** -0.5) * _LOG2_E),
        out_shape=jax.ShapeDtypeStruct((n_tok, d_attn), BF16),
        grid=(batch, n_heads),
        in_specs=[pl.BlockSpec((seq, HEAD_DIM), lambda b, h: (b, h)),
                  pl.BlockSpec((seq, HEAD_DIM), lambda b, h: (b, n_heads + h)),
                  pl.BlockSpec((seq, HEAD_DIM), lambda b, h: (b, 2 * n_heads + h)),
                  pl.BlockSpec((1, HEAD_DIM, 1), lambda b, h: (h, 0, 0))],
        out_specs=pl.BlockSpec((seq, HEAD_DIM), lambda b, h: (b, h)),
        scratch_shapes=[pltpu.VMEM((seq // tk, HEAD_DIM, tk), BF16),
                        pltpu.VMEM((tk, tq), F32), pltpu.VMEM((tk, tq), F32),
                        pltpu.VMEM((tk, tq), BF16), pltpu.VMEM((tk, tq), BF16)],
        compiler_params=_params(("parallel", "parallel")),
        name="stickbreak_attn",
    )(proj, proj, proj, g_head.reshape(n_heads, HEAD_DIM, 1))


def _pool_kernel(p_ref, halo_ref, w_ref, s_ref, o_ref, *, ts, seq_tiles, group):
    tile = pl.program_id(0) % seq_tiles
    p = p_ref[...].astype(F32)
    halo = jnp.where(tile == 0, 0.0, halo_ref[...].astype(F32))
    ext = jnp.concatenate([halo, p], axis=0)
    pos = tile * ts + lax.broadcasted_iota(I32, (ts, 1), 0)
    for gi, w in enumerate(POOL_WINDOWS):
        cols = slice(gi * group, (gi + 1) * group)
        tot = ext[:, cols]
        span = 1
        while span < w:
            tot = tot + pltpu.roll(tot, span, 0)
            span *= 2
        count = jnp.minimum(pos + 1, w).astype(F32)
        d = tot[POOL_HALO:] / count - p[:, cols]
        y = jnp.dot(d.astype(BF16), w_ref[gi], preferred_element_type=F32) * s_ref[:, cols]
        o_ref[:, cols] = y.astype(o_ref.dtype)


def _pool(proj, w_pool, s_pool, seq, col_block, ts=512):
    n_tok = proj.shape[0]
    n_groups, group, _ = w_pool.shape
    d_pool = n_groups * group
    seq_tiles = seq // ts
    halo_per_tile = ts // POOL_HALO
    return pl.pallas_call(
        functools.partial(_pool_kernel, ts=ts, seq_tiles=seq_tiles, group=group),
        out_shape=jax.ShapeDtypeStruct((n_tok, d_pool), BF16),
        grid=(n_tok // ts,),
        in_specs=[pl.BlockSpec((ts, d_pool), lambda i: (i, col_block)),
                  pl.BlockSpec((POOL_HALO, d_pool),
                               lambda i: (jnp.maximum(i * halo_per_tile - 1, 0), col_block)),
                  pl.BlockSpec((n_groups, group, group), lambda i: (0, 0, 0)),
                  pl.BlockSpec((1, d_pool), lambda i: (0, 0))],
        out_specs=pl.BlockSpec((ts, d_pool), lambda i: (i, 0)),
        compiler_params=_params(("parallel",)),
        name="multiscale_pool",
    )(proj, proj, w_pool, s_pool.reshape(1, d_pool))


def _out_proj_kernel(a1_ref, a2_ref, w1_ref, w2_ref, x_ref, gate_ref, o_ref):
    mix = (jnp.dot(a1_ref[...], w1_ref[...], preferred_element_type=F32)
           + jnp.dot(a2_ref[...], w2_ref[...], preferred_element_type=F32))
    o_ref[...] = x_ref[...] + gate_ref[0] * mix


def _out_proj(o_attn, o_pool, w_out, x2d, gate, seq, tm=512, tn=1024):
    m, d = x2d.shape
    k1, k2 = o_attn.shape[1], o_pool.shape[1]
    assert k1 == k2
    return pl.pallas_call(
        _out_proj_kernel,
        out_shape=jax.ShapeDtypeStruct((m, d), F32),
        grid=(m // tm, d // tn),
        in_specs=[pl.BlockSpec((tm, k1), lambda i, j: (i, 0)),
                  pl.BlockSpec((tm, k2), lambda i, j: (i, 0)),
                  pl.BlockSpec((k1, tn), lambda i, j: (0, j)),
                  pl.BlockSpec((k2, tn), lambda i, j: (1, j)),
                  pl.BlockSpec((tm, tn), lambda i, j: (i, j)),
                  pl.BlockSpec((1, 1, tn), lambda i, j: ((i * tm) // seq, 0, j))],
        out_specs=pl.BlockSpec((tm, tn), lambda i, j: (i, j)),
        compiler_params=_params(("parallel", "arbitrary")),
        name="out_proj",
    )(o_attn, o_pool, w_out, w_out, x2d, gate)


def _top_rows(vals, k):
    n_rows, width = vals.shape
    rows = lax.broadcasted_iota(I32, (n_rows, width), 0)
    slot = lax.broadcasted_iota(I32, (k, width), 0)
    out_v = jnp.zeros((k, width), F32)
    out_i = jnp.zeros((k, width), I32)
    for r in range(k):
        m = jnp.max(vals, axis=0, keepdims=True)
        idx = jnp.min(jnp.where(vals == m, rows, n_rows), axis=0, keepdims=True)
        out_v = jnp.where(slot == r, m, out_v)
        out_i = jnp.where(slot == r, idx, out_i)
        vals = jnp.where(rows == idx, -jnp.inf, vals)
    return out_v, out_i, vals


def _pick_rows(table, sel):
    out = jnp.zeros(sel.shape, table.dtype)
    for r in range(table.shape[0]):
        out = jnp.where(sel == r, table[r:r + 1], out)
    return out


def _staircase(v1, v2):
    width = v1.shape[1]
    r8 = lax.broadcasted_iota(I32, (MIN_SUBLANES, width), 0)
    ninf = jnp.float32(-jnp.inf)
    lead = v2[0:MIN_SUBLANES]
    groups = [
        v1[0:1] + v2,
        v1[1:2] + lead,
        jnp.where(r8 < 5, v1[2:3] + lead, ninf),
        jnp.where(r8 < 4, v1[3:4] + lead,
                  jnp.where(r8 < 7, v1[4:5] + pltpu.roll(lead, 4, 0), ninf)),
        jnp.where(r8 < 6,
                  jnp.where(r8 < 2, v1[5:6], jnp.where(r8 < 4, v1[6:7], v1[7:8]))
                  + jnp.where(r8 % 2 == 0, v2[0:1], v2[1:2]), ninf),
        v1[MIN_SUBLANES:] + v2[0:1],
    ]
    return jnp.concatenate(groups, axis=0)


def _staircase_coords(row):
    a = jnp.where(row < 16, 0, jnp.where(row < 24, 1, jnp.where(row < 32, 2, jnp.where(
        row < 36, 3, jnp.where(row < 40, 4, jnp.where(row < 48, 5 + (row - 40) // 2, row - 40))))))
    b = jnp.where(row < 16, row, jnp.where(row < 32, row % 8, jnp.where(
        row < 40, row % 4, jnp.where(row < 48, row % 2, 0))))
    return a, b


def _retrieve_kernel(q_ref, k1_ref, k2_ref,
                     s1_ref, c1_ref, s2_ref, e2_ref, tau_ref, tie_ref, idx_ref, gate_ref):
    half = PEER_NKEYS
    nt = (((1,), (1,)), ((), ()))
    for h in range(PEER_HEADS):
        q1 = q_ref[:, (2 * h) * half:(2 * h + 1) * half]
        q2 = q_ref[:, (2 * h + 1) * half:(2 * h + 2) * half]
        s1 = lax.dot_general(k1_ref[...], q1, nt, preferred_element_type=F32)
        s2 = lax.dot_general(k2_ref[...], q2, nt, preferred_element_type=F32)
        v1, i1, rest1 = _top_rows(s1, PEER_TOPK)
        v2, i2, rest2 = _top_rows(s2, PEER_TOPK)
        top_s, pos, rest_c = _top_rows(_staircase(v1, v2), PEER_TOPK)
        tau = top_s[PEER_TOPK - 1:PEER_TOPK]
        z = jnp.exp(top_s - top_s[0:1])
        denom = jnp.sum(z, axis=0, keepdims=True)
        next1 = jnp.max(rest1, axis=0, keepdims=True)
        next2 = jnp.max(rest2, axis=0, keepdims=True)
        next_c = jnp.max(rest_c, axis=0, keepdims=True)
        tie = (next_c >= tau) | (next1 + v2[0:1] >= tau) | (v1[0:1] + next2 >= tau)

        s1_ref[h] = s1
        s2_ref[h] = s2
        c1_ref[h] = jnp.exp(s1 - v1[0:1])
        e2_ref[h] = jnp.exp(s2 - v2[0:1]) / denom
        tau_ref[h:h + 1, :] = tau
        tie_ref[h:h + 1, :] = jnp.where(tie, 1, 0).astype(I32)
        a, b = _staircase_coords(pos)
        idx_ref[h] = _pick_rows(i1, a) * PEER_NKEYS + _pick_rows(i2, b)
        gate_ref[h] = z / denom


def _retrieve(q, keys1, keys2, tt=256):
    n_tok, qw = q.shape
    kshape = (PEER_HEADS, PEER_NKEYS, n_tok)
    lshape = (PEER_HEADS, PEER_TOPK, n_tok)
    kspec = pl.BlockSpec((PEER_HEADS, PEER_NKEYS, tt), lambda i: (0, 0, i))
    lspec = pl.BlockSpec((PEER_HEADS, PEER_TOPK, tt), lambda i: (0, 0, i))
    hspec = pl.BlockSpec((PEER_HEADS, tt), lambda i: (0, i))
    return pl.pallas_call(
        _retrieve_kernel,
        out_shape=(jax.ShapeDtypeStruct(kshape, F32), jax.ShapeDtypeStruct(kshape, F32),
                   jax.ShapeDtypeStruct(kshape, F32), jax.ShapeDtypeStruct(kshape, F32),
                   jax.ShapeDtypeStruct((PEER_HEADS, n_tok), F32),
                   jax.ShapeDtypeStruct((PEER_HEADS, n_tok), I32),
                   jax.ShapeDtypeStruct(lshape, I32), jax.ShapeDtypeStruct(lshape, F32)),
        grid=(n_tok // tt,),
        in_specs=[pl.BlockSpec((tt, qw), lambda i: (i, 0)),
                  pl.BlockSpec((PEER_NKEYS, PEER_NKEYS), lambda i: (0, 0)),
                  pl.BlockSpec((PEER_NKEYS, PEER_NKEYS), lambda i: (0, 0))],
        out_specs=(kspec, kspec, kspec, kspec, hspec, hspec, lspec, lspec),
        compiler_params=_params(("parallel",)),
        name="peer_retrieve",
    )(q, keys1, keys2)


def _gelu(x):
    return 0.5 * x * (1.0 + lax.erf(x * (2.0 ** -0.5)))


def _peer_kernel(tie_ref, h_ref, u_ref, vt_ref, s1_ref, c1_ref, s2_ref, e2_ref, tau_ref,
                 idx_ref, gate_ref, o_ref, *, n_i, i_per_chunk, tok_chunk):
    t = pl.program_id(0)
    e = pl.program_id(1)
    width = h_ref.shape[0]

    @pl.when(e == 0)
    def _():
        o_ref[...] = jnp.zeros_like(o_ref)

    def gates_by_threshold(ii, tok):
        w = jnp.zeros((PEER_NKEYS, tok.stop - tok.start), F32)
        for h in range(PEER_HEADS):
            score = s2_ref[h, :, tok] + s1_ref[ii, h:h + 1, tok]
            w = w + (jnp.where(score >= tau_ref[h:h + 1, tok], e2_ref[h, :, tok], 0.0)
                     * c1_ref[ii, h:h + 1, tok])
        return w

    def gates_by_list(ii, tok):
        cols = tok.stop - tok.start
        ids = ((e * n_i + ii) * PEER_NKEYS
               + lax.broadcasted_iota(I32, (PEER_NKEYS, cols), 0))
        w = jnp.zeros((PEER_NKEYS, cols), F32)
        for h in range(PEER_HEADS):
            def slot(k, w):
                hit = ids == idx_ref[h, pl.ds(k, 1), tok]
                return w + jnp.where(hit, gate_ref[h, pl.ds(k, 1), tok], 0.0)
            w = lax.fori_loop(0, PEER_TOPK, slot, w)
        return w

    def step(gates):
        n_ec = n_i // i_per_chunk
        rows = i_per_chunk * PEER_NKEYS
        units = [(ec, tc) for tc in range(width // tok_chunk) for ec in range(n_ec)]

        def scores(ec, tc):
            return lax.dot_general(u_ref[ec * rows:(ec + 1) * rows, :],
                                   h_ref[tc * tok_chunk:(tc + 1) * tok_chunk, :],
                                   (((1,), (1,)), ((), ())), preferred_element_type=F32)

        def activations(s, ec, tc):
            tok = slice(tc * tok_chunk, (tc + 1) * tok_chunk)
            return jnp.concatenate(
                [(_gelu(s[ii * PEER_NKEYS:(ii + 1) * PEER_NKEYS])
                  * gates(ec * i_per_chunk + ii, tok)).astype(BF16)
                 for ii in range(i_per_chunk)], axis=0)

        def project(act, ec):
            return jnp.dot(vt_ref[0, :, ec * rows:(ec + 1) * rows], act,
                           preferred_element_type=F32)

        s_next = scores(*units[0])
        act_prev = None
        partial = None
        for c, (ec, tc) in enumerate(units):
            s_cur = s_next
            if c + 1 < len(units):
                s_next = scores(*units[c + 1])
            act = activations(s_cur, ec, tc)
            for done_act, (dec, dtc) in ([(act_prev, units[c - 1])] if c else []) + (
                    [(act, (ec, tc))] if c + 1 == len(units) else []):
                part = project(done_act, dec)
                partial = part if partial is None else partial + part
                if dec == n_ec - 1:
                    tok = slice(dtc * tok_chunk, (dtc + 1) * tok_chunk)
                    o_ref[:, tok] += partial
                    partial = None
            act_prev = act

    @pl.when(tie_ref[t] == 0)
    def _():
        step(gates_by_threshold)

    @pl.when(tie_ref[t] != 0)
    def _():
        step(gates_by_list)


def _peer(tile_ties, h2, u_bf, vt_tiles, s1, c1, s2, e2, tau, idx, gate):
    n_tok, d = h2.shape
    n_blocks, _, te = vt_tiles.shape
    tt = PEER_TT
    n_i = te // PEER_NKEYS
    once = dict(pipeline_mode=pl.Buffered(1))
    kspec = pl.BlockSpec((PEER_HEADS, PEER_NKEYS, tt), lambda t, e, ties: (0, 0, t), **once)
    ispec = pl.BlockSpec((n_i, PEER_HEADS, tt), lambda t, e, ties: (e, 0, t))
    lspec = pl.BlockSpec((PEER_HEADS, PEER_TOPK, tt), lambda t, e, ties: (0, 0, t), **once)
    grid_spec = pltpu.PrefetchScalarGridSpec(
        num_scalar_prefetch=1,
        grid=(n_tok // tt, n_blocks),
        in_specs=[pl.BlockSpec((tt, d), lambda t, e, ties: (t, 0), **once),
                  pl.BlockSpec((te, d), lambda t, e, ties: (e, 0)),
                  pl.BlockSpec((1, d, te), lambda t, e, ties: (e, 0, 0)),
                  ispec, ispec, kspec, kspec,
                  pl.BlockSpec((PEER_HEADS, tt), lambda t, e, ties: (0, t), **once),
                  lspec, lspec],
        out_specs=pl.BlockSpec((d, tt), lambda t, e, ties: (0, t), **once),
    )
    return pl.pallas_call(
        functools.partial(_peer_kernel, n_i=n_i, i_per_chunk=min(n_i, PEER_CHUNK_KEYS),
                          tok_chunk=min(tt, PEER_TOK_CHUNK)),
        out_shape=jax.ShapeDtypeStruct((d, n_tok), F32),
        grid_spec=grid_spec,
        compiler_params=_params(("parallel", "arbitrary")),
        name="peer_experts",
    )(tile_ties, h2, u_bf, vt_tiles, s1, c1, s2, e2, tau, idx, gate)


def _final_kernel(x_ref, yt_ref, gate_ref, g_ref, o_ref):
    x = x_ref[...] + gate_ref[0] * yt_ref[...].T
    o_ref[...] = x * lax.rsqrt(jnp.mean(x * x, axis=-1, keepdims=True) + EPS) * g_ref[...]


def _final(x1, y_t, gate, g_final, seq, ts=256):
    m, d = x1.shape
    return pl.pallas_call(
        _final_kernel,
        out_shape=jax.ShapeDtypeStruct((m, d), F32),
        grid=(m // ts,),
        in_specs=[pl.BlockSpec((ts, d), lambda i: (i, 0)),
                  pl.BlockSpec((d, ts), lambda i: (0, i)),
                  pl.BlockSpec((1, 1, d), lambda i: ((i * ts) // seq, 0, 0)),
                  pl.BlockSpec((1, d), lambda i: (0, 0))],
        out_specs=pl.BlockSpec((ts, d), lambda i: (i, 0)),
        compiler_params=_params(("parallel",)),
        name="final_norm",
    )(x1, y_t, gate, g_final.reshape(1, d))


def _layer(x, silu_mod, g_norm1, w_in, g_attn_head, w_pool, s_pool, w_out, g_norm2, w_query,
           sub_keys_1, sub_keys_2, u_experts, v_experts):
    b, s, d = x.shape
    n_tok = b * s
    n_heads = g_attn_head.shape[0]
    d_attn = n_heads * HEAD_DIM
    mod = silu_mod.reshape(b, N_MOD, 1, d)
    shift1, scale1, gate1, shift2, scale2, gate2 = (mod[:, i] for i in range(N_MOD))

    h = _norm_mod(x, g_norm1, scale1, shift1).reshape(n_tok, d)
    proj = _matmul(h, w_in.astype(BF16), tm=1024, tn=512, name="in_proj")
    o_attn = _attention(proj, g_attn_head, b, s, n_heads)
    o_pool = _pool(proj, w_pool.astype(BF16), s_pool, s, col_block=(3 * d_attn) // (d - d_attn))
    x1 = _out_proj(o_attn, o_pool, w_out.astype(BF16), x.reshape(n_tok, d), gate1, s)

    h2 = _norm_mod(x1.reshape(b, s, d), g_norm2, scale2, shift2).reshape(n_tok, d)
    q = _matmul(h2, w_query.astype(BF16), tm=1024, tn=512, name="peer_query")
    s1, c1, s2, e2, tau, tie, idx, gate = _retrieve(
        q, sub_keys_1.astype(BF16), sub_keys_2.astype(BF16))
    tile_ties = jnp.max(tie.reshape(PEER_HEADS, n_tok // PEER_TT, PEER_TT), axis=(0, 2))
    n_exp = v_experts.shape[0]
    vt_tiles = jnp.transpose(v_experts.reshape(n_exp // PEER_TE, PEER_TE, d),
                             (0, 2, 1)).astype(BF16)
    y_t = _peer(tile_ties, h2, u_experts.astype(BF16), vt_tiles,
                jnp.transpose(s1, (1, 0, 2)), jnp.transpose(c1, (1, 0, 2)), s2, e2, tau,
                idx, gate)
    return x1, y_t, gate2


def kernel(x, c, w_ada, b_ada, g_norm1, w_in, g_attn_head, w_pool, s_pool, w_out, g_norm2,
           w_query, sub_keys_1, sub_keys_2, u_experts, v_experts, g_final):
    b, s, d = x.shape
    depth = w_ada.shape[0]
    assert depth == 1, "the fused final norm assumes a single layer"
    mod = _ada_mod(c, w_ada[0], b_ada[0])
    x1, y_t, gate2 = _layer(x, mod, g_norm1[0], w_in[0], g_attn_head[0], w_pool[0], s_pool[0],
                            w_out[0], g_norm2[0], w_query[0], sub_keys_1[0], sub_keys_2[0],
                            u_experts[0], v_experts[0])
    out = _final(x1, y_t, gate2, g_final, s)
    return out.reshape(b, s, d)
```
